```python
import jax, jax.numpy as jnp
from jax import lax
import numpy as np

D_MODEL = 1024
BATCH = 8
SEQ = 8192
DEPTH = 1

HEAD_DIM = 64
ROT_DIM = HEAD_DIM // 4
ROPE_THETA = 500000.0
BLOCK = 128
A_Q_HEADS = 16
A_KV_HEADS = 4
A_WINDOW = 128
B_GROUPS = ((128, 1), (512, 4), (2048, 16))
B_NG = len(B_GROUPS)
B_HEADS_PER_GROUP = 4
B_HEADS = B_NG * B_HEADS_PER_GROUP
A_Q_W = A_Q_HEADS * HEAD_DIM
A_KV_W = A_KV_HEADS * HEAD_DIM
B_W = B_HEADS * HEAD_DIM
SPLITS = (A_Q_W, A_Q_W + A_KV_W, A_Q_W + 2 * A_KV_W, A_Q_W + 2 * A_KV_W + B_W, A_Q_W + 2 * A_KV_W + 2 * B_W)
IN_W = A_Q_W + 2 * A_KV_W + 3 * B_W
A_OUT_W = A_Q_W
B_OUT_W = B_HEADS_PER_GROUP * HEAD_DIM
N_BRANCH = 2
D_FF = ((8 * D_MODEL // 3 + 255) // 256) * 256
DN_ALPHA = (2 * DEPTH) ** 0.25
DN_BETA = (8 * DEPTH) ** -0.25
LN_EPS = 1e-5
NEG = -1e30

kernel_name = "hybrid_gated_window_dilated_encoder_layer"


def layer_norm(x, g, b):
    xf = x.astype(jnp.float32)
    mu = xf.mean(-1, keepdims=True)
    var = jnp.square(xf - mu).mean(-1, keepdims=True)
    return ((xf - mu) * lax.rsqrt(var + LN_EPS) * g.astype(jnp.float32) + b.astype(jnp.float32)).astype(x.dtype)


def rope_partial(t, pos):
    half = ROT_DIM // 2
    inv = ROPE_THETA ** (-jnp.arange(half, dtype=jnp.float32) / half)
    ang = pos.astype(jnp.float32)[:, None] * inv[None, :]
    cos = jnp.cos(ang)[None, :, None, :].astype(t.dtype)
    sin = jnp.sin(ang)[None, :, None, :].astype(t.dtype)
    t1, t2, tp = t[..., :half], t[..., half:ROT_DIM], t[..., ROT_DIM:]
    return jnp.concatenate([t1 * cos - t2 * sin, t1 * sin + t2 * cos, tp], axis=-1)


def windowed_gqa_sink(q, k, v, sink):
    b, s, hq, dh = q.shape
    hkv = k.shape[2]
    g = hq // hkv
    nb = s // BLOCK
    pad = ((0, 0), (BLOCK, BLOCK), (0, 0), (0, 0))
    kp = jnp.pad(k, pad)
    vp = jnp.pad(v, pad)
    qg = q.reshape(b, s, hkv, g, dh)
    sk = sink.astype(jnp.float32).reshape(1, hkv, g, 1)
    scale = dh ** -0.5

    def block(i):
        start = i * BLOCK
        qb = lax.dynamic_slice_in_dim(qg, start, BLOCK, axis=1)
        kb = lax.dynamic_slice_in_dim(kp, start, 3 * BLOCK, axis=1)
        vb = lax.dynamic_slice_in_dim(vp, start, 3 * BLOCK, axis=1)
        qpos = start + jnp.arange(BLOCK)
        kpos = start - BLOCK + jnp.arange(3 * BLOCK)
        valid = (jnp.abs(kpos[None, :] - qpos[:, None]) <= A_WINDOW) & ((kpos >= 0) & (kpos < s))[None, :]
        sc = jnp.einsum('bqkgd,bskd->bkgqs', qb, kb).astype(jnp.float32) * scale
        sc = jnp.where(valid, sc, NEG)
        m = jnp.maximum(sc.max(-1), sk)
        p = jnp.exp(sc - m[..., None])
        den = p.sum(-1) + jnp.exp(sk - m)
        o = jnp.einsum('bkgqs,bskd->bqkgd', p, vb.astype(jnp.float32))
        o = o / den.transpose(0, 3, 1, 2)[..., None]
        return o.reshape(b, BLOCK, hq * dh)

    out = lax.map(block, jnp.arange(nb))
    return out.transpose(1, 0, 2, 3).reshape(b, s, hq * dh)


def dilated_mixture(q, k, v):
    b, s, ng, hg, dh = q.shape
    nb = s // BLOCK
    scale = dh ** -0.5
    qs = [q[:, :, gi] for gi in range(ng)]
    ks = [k[:, :, gi] for gi in range(ng)]
    vs = [v[:, :, gi] for gi in range(ng)]

    def block(i):
        start = i * BLOCK
        qpos = start + jnp.arange(BLOCK)
        outs, maxs, dens = [], [], []
        for gi, (w, d) in enumerate(B_GROUPS):
            n_side = (w // 2) // d
            offs = jnp.arange(-n_side, n_side + 1) * d
            idx = qpos[:, None] + offs[None, :]
            valid = (idx >= 0) & (idx < s)
            idx_c = jnp.clip(idx, 0, s - 1)
            qb = lax.dynamic_slice_in_dim(qs[gi], start, BLOCK, axis=1)
            kb = jnp.take(ks[gi], idx_c, axis=1)
            vb = jnp.take(vs[gi], idx_c, axis=1)
            sc = jnp.einsum('bqhd,bqjhd->bhqj', qb, kb).astype(jnp.float32) * scale
            sc = jnp.where(valid[None, None], sc, NEG)
            m = sc.max(-1)
            p = jnp.exp(sc - m[..., None])
            l = p.sum(-1)
            o = jnp.einsum('bhqj,bqjhd->bqhd', p, vb.astype(jnp.float32)) / l.transpose(0, 2, 1)[..., None]
            outs.append(o)
            maxs.append(m)
            dens.append(l)
        m_all = jnp.stack(maxs)
        l_all = jnp.stack(dens)
        wts = l_all * jnp.exp(m_all - m_all.max(0, keepdims=True))
        wts = wts / wts.sum(0, keepdims=True)
        o = (wts.transpose(0, 1, 3, 2)[..., None] * jnp.stack(outs)).sum(0)
        return o.reshape(b, BLOCK, hg * dh)

    out = lax.map(block, jnp.arange(nb))
    return out.transpose(1, 0, 2, 3).reshape(b, s, hg * dh)


def setup_inputs(seed: int = 0) -> dict:
    key = jax.random.key(seed)
    ks = jax.random.split(key, 16)
    f32 = jnp.float32
    n = lambda k, shape, sc: jax.random.normal(k, shape, f32) * sc
    x = jax.random.normal(ks[0], (BATCH, SEQ, D_MODEL), f32)
    w_in = n(ks[1], (DEPTH, D_MODEL, IN_W), D_MODEL ** -0.5)
    col_scale = np.ones((IN_W,), np.float32)
    col_scale[SPLITS[1]:SPLITS[2]] = DN_BETA
    col_scale[SPLITS[4]:] = DN_BETA
    w_in = w_in * jnp.asarray(col_scale)
    a_sink = n(ks[2], (DEPTH, A_Q_HEADS), 0.5)
    w_gate = n(ks[3], (DEPTH, D_MODEL, N_BRANCH * D_MODEL), D_MODEL ** -0.5)
    b_gate = n(ks[4], (DEPTH, N_BRANCH * D_MODEL), 0.1)
    w_br_a = n(ks[5], (DEPTH, A_OUT_W, D_MODEL), A_OUT_W ** -0.5)
    w_br_b = n(ks[6], (DEPTH, B_OUT_W, D_MODEL), B_OUT_W ** -0.5)
    w_out = n(ks[7], (DEPTH, D_MODEL, D_MODEL), DN_BETA * D_MODEL ** -0.5)
    ln1_g = 1.0 + n(ks[8], (DEPTH, D_MODEL), 0.02)
    ln1_b = n(ks[9], (DEPTH, D_MODEL), 0.02)
    w_ff_gate = n(ks[10], (DEPTH, D_MODEL, D_FF), D_MODEL ** -0.5)
    w_ff_up = n(ks[11], (DEPTH, D_MODEL, D_FF), D_MODEL ** -0.5)
    w_ff_down = n(ks[12], (DEPTH, D_FF, D_MODEL), DN_BETA * D_FF ** -0.5)
    ln2_g = 1.0 + n(ks[13], (DEPTH, D_MODEL), 0.02)
    ln2_b = n(ks[14], (DEPTH, D_MODEL), 0.02)
    return {"x": x, "w_in": w_in, "a_sink": a_sink, "w_gate": w_gate, "b_gate": b_gate,
            "w_br_a": w_br_a, "w_br_b": w_br_b, "w_out": w_out, "ln1_g": ln1_g, "ln1_b": ln1_b,
            "w_ff_gate": w_ff_gate, "w_ff_up": w_ff_up, "w_ff_down": w_ff_down,
            "ln2_g": ln2_g, "ln2_b": ln2_b}


def reference(x, w_in, a_sink, w_gate, b_gate, w_br_a, w_br_b, w_out, ln1_g, ln1_b,
              w_ff_gate, w_ff_up, w_ff_down, ln2_g, ln2_b):
    b, s, _ = x.shape
    pos = jnp.arange(s)
    h = x
    for l in range(DEPTH):
        proj = h @ w_in[l]
        qa, ka, va, qb, kb, vb = jnp.split(proj, SPLITS, axis=-1)
        qa = rope_partial(qa.reshape(b, s, A_Q_HEADS, HEAD_DIM), pos)
        ka = rope_partial(ka.reshape(b, s, A_KV_HEADS, HEAD_DIM), pos)
        va = va.reshape(b, s, A_KV_HEADS, HEAD_DIM)
        qb = rope_partial(qb.reshape(b, s, B_HEADS, HEAD_DIM), pos).reshape(b, s, B_NG, B_HEADS_PER_GROUP, HEAD_DIM)
        kb = rope_partial(kb.reshape(b, s, B_HEADS, HEAD_DIM), pos).reshape(b, s, B_NG, B_HEADS_PER_GROUP, HEAD_DIM)
        vb = vb.reshape(b, s, B_NG, B_HEADS_PER_GROUP, HEAD_DIM)
        o_a = windowed_gqa_sink(qa, ka, va, a_sink[l]).astype(h.dtype)
        o_b = dilated_mixture(qb, kb, vb).astype(h.dtype)
        gates = jax.nn.sigmoid(h @ w_gate[l] + b_gate[l])
        g_a, g_b = jnp.split(gates, N_BRANCH, axis=-1)
        mixed = g_a * (o_a @ w_br_a[l]) + g_b * (o_b @ w_br_b[l])
        y = mixed @ w_out[l]
        h = layer_norm(DN_ALPHA * h + y, ln1_g[l], ln1_b[l])
        f = (jax.nn.silu(h @ w_ff_gate[l]) * (h @ w_ff_up[l])) @ w_ff_down[l]
        h = layer_norm(DN_ALPHA * h + f, ln2_g[l], ln2_b[l])
    return h
```

```python
import functools

import numpy as np
import jax
import jax.numpy as jnp
from jax import lax
from jax.experimental import pallas as pl
from jax.experimental.pallas import tpu as pltpu

D_MODEL = 1024
HEAD_DIM = 64
ROT_DIM = HEAD_DIM // 4
ROPE_THETA = 500000.0
BLOCK = 128
A_Q_HEADS = 16
A_KV_HEADS = 4
A_WINDOW = 128
B_GROUPS = ((128, 1), (512, 4), (2048, 16))
B_HEADS_PER_GROUP = 4
B_SIDE = 64
A_Q_W = A_Q_HEADS * HEAD_DIM
A_KV_W = A_KV_HEADS * HEAD_DIM
B_GW = B_HEADS_PER_GROUP * HEAD_DIM
B_W = len(B_GROUPS) * B_GW
OFF_KA = A_Q_W
OFF_VA = OFF_KA + A_KV_W
OFF_QB = OFF_VA + A_KV_W
OFF_KB = OFF_QB + B_W
OFF_VB = OFF_KB + B_W
DEPTH = 1
DN_ALPHA = (2 * DEPTH) ** 0.25
LN_EPS = 1e-5
NEG = -1e30
LANES = 128
PAIR = 2 * HEAD_DIM
VMEM_LIMIT = 56 * 1024 * 1024

TM_PROJ = 512
TQ_A = 1024
TS_B = 2048
TM_DENSE = 512

_BF16 = jnp.bfloat16
_F32 = jnp.float32


def _const_spec(shape):
    nd = len(shape)
    return pl.BlockSpec(shape, lambda *_: (0,) * nd, pipeline_mode=pl.Buffered(1))


def _lane_lt64(rows):
    return lax.broadcasted_iota(jnp.int32, (rows, LANES), 1) < HEAD_DIM


def _rope(t, cos, sa, sb):
    return t * cos + pltpu.roll(t, LANES - ROT_DIM // 2, 1) * sa + pltpu.roll(t, ROT_DIM // 2, 1) * sb


def _proj_kernel(x_ref, w_ref, cos_ref, sa_ref, sb_ref,
                 qa_ref, ka_ref, va_ref,
                 q0_ref, k0_ref, v0_ref, q1_ref, k1_ref, v1_ref, q2_ref, k2_ref, v2_ref,
                 slab_ref):
    tm = x_ref.shape[1]
    xb = x_ref[0].astype(_BF16)
    cos = cos_ref[...]
    sa = sa_ref[...]
    sb = sb_ref[...]
    scale = HEAD_DIM ** -0.5
    lt64 = _lane_lt64(tm)

    def mm(c0, width=2 * LANES):
        return jnp.dot(xb, w_ref[:, c0:c0 + width], preferred_element_type=_F32)

    for c in range(A_Q_W // (2 * LANES)):
        r = mm(c * 2 * LANES)
        for h in range(2):
            t = _rope(r[:, h * LANES:(h + 1) * LANES], cos, sa, sb) * scale
            qa_ref[0, :, c * 2 * LANES + h * LANES:c * 2 * LANES + (h + 1) * LANES] = t.astype(_BF16)

    for off, out_ref, rot in ((OFF_KA, ka_ref, True), (OFF_VA, va_ref, False)):
        r = mm(off)
        for h in range(2):
            t = r[:, h * LANES:(h + 1) * LANES]
            if rot:
                t = _rope(t, cos, sa, sb)
            sw = pltpu.roll(t, HEAD_DIM, 1)
            out_ref[0, :, (2 * h) * LANES:(2 * h + 1) * LANES] = jnp.where(lt64, t, sw).astype(_BF16)
            out_ref[0, :, (2 * h + 1) * LANES:(2 * h + 2) * LANES] = jnp.where(lt64, sw, t).astype(_BF16)

    outs = ((q0_ref, k0_ref, v0_ref), (q1_ref, k1_ref, v1_ref), (q2_ref, k2_ref, v2_ref))
    for g, (_, dil) in enumerate(B_GROUPS):
        for kind, off in enumerate((OFF_QB, OFF_KB, OFF_VB)):
            out_ref = outs[g][kind]
            r = mm(off + g * B_GW)
            for h in range(2):
                t = r[:, h * LANES:(h + 1) * LANES]
                if kind < 2:
                    t = _rope(t, cos, sa, sb)
                if kind == 0:
                    t = t * scale
                if dil == 1:
                    out_ref[0, 0, :, h * LANES:(h + 1) * LANES] = t.astype(_BF16)
                else:
                    slab_ref[h] = t
            if dil > 1:
                rows = tm // dil
                for h in range(2):
                    for p in range(dil):
                        v = slab_ref[h, pl.ds(p, rows, stride=dil), :]
                        out_ref[0, p, :, h * LANES:(h + 1) * LANES] = v.astype(_BF16)


def _proj(x, w_in_bf, cos_t, sa_t, sb_t):
    b, s, _ = x.shape
    tm = TM_PROJ
    grid = (s // tm, b)
    tok = lambda i, j: (j, i, 0)
    tab = pl.BlockSpec((tm, LANES), lambda i, j: (i, 0))
    out_shape = [jax.ShapeDtypeStruct((b, s, A_Q_W), _BF16),
                 jax.ShapeDtypeStruct((b, s, 2 * A_KV_W), _BF16),
                 jax.ShapeDtypeStruct((b, s, 2 * A_KV_W), _BF16)]
    out_specs = [pl.BlockSpec((1, tm, A_Q_W), tok),
                 pl.BlockSpec((1, tm, 2 * A_KV_W), tok),
                 pl.BlockSpec((1, tm, 2 * A_KV_W), tok)]
    for _, dil in B_GROUPS:
        for _ in range(3):
            out_shape.append(jax.ShapeDtypeStruct((b, dil, s // dil, B_GW), _BF16))
            out_specs.append(pl.BlockSpec((1, dil, tm // dil, B_GW), lambda i, j: (j, 0, i, 0)))
    return pl.pallas_call(
        _proj_kernel,
        out_shape=out_shape,
        grid=grid,
        in_specs=[pl.BlockSpec((1, tm, D_MODEL), tok), _const_spec(w_in_bf.shape), tab, tab, tab],
        out_specs=out_specs,
        scratch_shapes=[pltpu.VMEM((2, tm, LANES), _F32)],
        compiler_params=pltpu.CompilerParams(
            dimension_semantics=("arbitrary", "arbitrary"), vmem_limit_bytes=VMEM_LIMIT),
        name="proj",
    )(x, w_in_bf, cos_t, sa_t, sb_t)


def _attn_a_kernel(sink_ref, q_ref, kp_ref, kc_ref, kn_ref, vp_ref, vc_ref, vn_ref, band_ref,
                   o_ref, kbuf, vbuf):
    tq = q_ref.shape[1]
    seq = pl.num_programs(1) * tq
    t = pl.program_id(1)
    kbuf[0:BLOCK] = kp_ref[0]
    kbuf[BLOCK:BLOCK + tq] = kc_ref[0]
    kbuf[BLOCK + tq:2 * BLOCK + tq] = kn_ref[0]
    vbuf[0:BLOCK] = vp_ref[0]
    vbuf[BLOCK:BLOCK + tq] = vc_ref[0]
    vbuf[BLOCK + tq:2 * BLOCK + tq] = vn_ref[0]
    nkeys = 3 * BLOCK
    grp = A_Q_HEADS // A_KV_HEADS
    lt64_q = _lane_lt64(BLOCK)
    lt64_k = _lane_lt64(nkeys)
    zq = jnp.zeros((BLOCK, LANES), _BF16)
    zk = jnp.zeros((nkeys, LANES), _BF16)

    def block(j, carry):
        r0 = pl.multiple_of(j * BLOCK, BLOCK)
        kpos = t * tq + r0 - BLOCK + lax.broadcasted_iota(jnp.int32, (1, nkeys), 1)
        edge = jnp.where((kpos >= 0) & (kpos < seq), 0.0, NEG).astype(_F32)
        bias = band_ref[...] + edge
        for k in range(A_KV_HEADS):
            kd = kbuf[pl.ds(r0, nkeys), k * LANES:(k + 1) * LANES]
            vd = vbuf[pl.ds(r0, nkeys), k * LANES:(k + 1) * LANES]
            qs = []
            for pr in range(2):
                qp = q_ref[0, pl.ds(r0, BLOCK), (2 * k + pr) * LANES:(2 * k + pr + 1) * LANES]
                qs.append(jnp.where(lt64_q, qp, zq))
                qs.append(jnp.where(lt64_q, zq, qp))
            qst = jnp.concatenate(qs, axis=0)
            sc = lax.dot_general(qst, kd, (((1,), (1,)), ((), ())), preferred_element_type=_F32)
            ps, ls = [], []
            for i in range(grp):
                si = sc[i * BLOCK:(i + 1) * BLOCK] + bias
                sk = sink_ref[grp * k + i]
                m = jnp.maximum(jnp.max(si, axis=1, keepdims=True), sk)
                e = jnp.exp(si - m)
                ls.append(jnp.sum(e, axis=1, keepdims=True) + jnp.exp(sk - m))
                ps.append(e.astype(_BF16))
            pst = jnp.concatenate([jnp.concatenate([ps[0], ps[1]], axis=1),
                                   jnp.concatenate([ps[2], ps[3]], axis=1)], axis=0)
            vbd = jnp.concatenate([jnp.where(lt64_k, vd, zk), jnp.where(lt64_k, zk, vd)], axis=0)
            o = jnp.dot(pst, vbd, preferred_element_type=_F32)
            for pr in range(2):
                den = jnp.where(lt64_q, ls[2 * pr], ls[2 * pr + 1])
                res = o[pr * BLOCK:(pr + 1) * BLOCK] / den
                o_ref[0, pl.ds(r0, BLOCK), (2 * k + pr) * LANES:(2 * k + pr + 1) * LANES] = res.astype(o_ref.dtype)
        return carry

    lax.fori_loop(0, tq // BLOCK, block, 0)


def _attn_a(qa, ka2, va2, sink, band):
    b, s, _ = qa.shape
    tq = TQ_A
    nb = tq // BLOCK
    last = s // BLOCK - 1
    cur = lambda i, j: (i, j, 0)
    prev = lambda i, j: (i, jnp.maximum(j * nb - 1, 0), 0)
    nxt = lambda i, j: (i, jnp.minimum((j + 1) * nb, last), 0)
    kvw = ka2.shape[-1]
    kv_specs = [pl.BlockSpec((1, BLOCK, kvw), prev), pl.BlockSpec((1, tq, kvw), cur),
                pl.BlockSpec((1, BLOCK, kvw), nxt)]
    return pl.pallas_call(
        _attn_a_kernel,
        out_shape=jax.ShapeDtypeStruct((b, s, A_Q_W), _BF16),
        grid=(b, s // tq),
        in_specs=[pl.BlockSpec(memory_space=pltpu.SMEM),
                  pl.BlockSpec((1, tq, A_Q_W), cur)] + kv_specs + kv_specs
                 + [_const_spec(band.shape)],
        out_specs=pl.BlockSpec((1, tq, A_Q_W), cur),
        scratch_shapes=[pltpu.VMEM((tq + 2 * BLOCK, kvw), _BF16),
                        pltpu.VMEM((tq + 2 * BLOCK, kvw), _BF16)],
        compiler_params=pltpu.CompilerParams(
            dimension_semantics=("arbitrary", "arbitrary"), vmem_limit_bytes=VMEM_LIMIT),
        name="attn_a",
    )(sink, qa, ka2, ka2, ka2, va2, va2, va2, band)


_SLAB_O, _SLAB_M, _SLAB_L = 0, 1, 2


def _attn_b_kernel(*refs):
    ng = len(B_GROUPS)
    in_refs = refs[:7 * ng]
    band_ref = refs[7 * ng]
    o_ref = refs[7 * ng + 1]
    kbufs = refs[7 * ng + 2:7 * ng + 2 + ng]
    vbufs = refs[7 * ng + 2 + ng:7 * ng + 2 + 2 * ng]
    slab_ref = refs[7 * ng + 2 + 2 * ng]
    ts = o_ref.shape[1]
    t = pl.program_id(1)
    nkeys = BLOCK + 2 * B_SIDE
    lt64_q = _lane_lt64(BLOCK)
    lt64_k = _lane_lt64(nkeys)
    zq = jnp.zeros((BLOCK, LANES), _BF16)
    zk = jnp.zeros((nkeys, LANES), _BF16)

    for g in range(ng):
        _, kp_ref, kc_ref, kn_ref, vp_ref, vc_ref, vn_ref = in_refs[7 * g:7 * g + 7]
        nt = kc_ref.shape[2]
        for buf, pr, cr, nr in ((kbufs[g], kp_ref, kc_ref, kn_ref), (vbufs[g], vp_ref, vc_ref, vn_ref)):
            buf[:, 0:B_SIDE] = pr[0]
            buf[:, B_SIDE:B_SIDE + nt] = cr[0]
            buf[:, B_SIDE + nt:2 * B_SIDE + nt] = nr[0]

    def unit(g, p, j):
        q_ref = in_refs[7 * g]
        dil = B_GROUPS[g][1]
        nt = ts // dil
        nrows = pl.num_programs(1) * nt
        r0 = pl.multiple_of(j * BLOCK, BLOCK)
        krow = t * nt + r0 - B_SIDE + lax.broadcasted_iota(jnp.int32, (1, nkeys), 1)
        edge = jnp.where((krow >= 0) & (krow < nrows), 0.0, NEG).astype(_F32)
        bias = band_ref[...] + edge
        bias2 = jnp.concatenate([bias, bias], axis=0)
        res = []
        for pr in range(2):
            cols = slice(pr * LANES, (pr + 1) * LANES)
            qp = q_ref[0, p, pl.ds(r0, BLOCK), cols]
            kp = kbufs[g][p, pl.ds(r0, nkeys), cols]
            vp = vbufs[g][p, pl.ds(r0, nkeys), cols]
            qst = jnp.concatenate([jnp.where(lt64_q, qp, zq), jnp.where(lt64_q, zq, qp)], axis=0)
            sc = lax.dot_general(qst, kp, (((1,), (1,)), ((), ())), preferred_element_type=_F32) + bias2
            m = jnp.max(sc, axis=1, keepdims=True)
            e = jnp.exp(sc - m)
            l = jnp.sum(e, axis=1, keepdims=True)
            eb = e.astype(_BF16)
            pcat = jnp.concatenate([eb[:BLOCK], eb[BLOCK:]], axis=1)
            vbd = jnp.concatenate([jnp.where(lt64_k, vp, zk), jnp.where(lt64_k, zk, vp)], axis=0)
            o = jnp.dot(pcat, vbd, preferred_element_type=_F32)
            mb = jnp.where(lt64_q, m[:BLOCK], m[BLOCK:])
            lb = jnp.where(lt64_q, l[:BLOCK], l[BLOCK:])
            res.append((o, mb, lb))
        return res

    for g in range(1, ng):
        dil = B_GROUPS[g][1]
        nj = ts // dil // BLOCK

        def body(u, carry, g=g, dil=dil, nj=nj):
            p = u // nj
            j = u % nj
            res = unit(g, p, j)
            start = p + j * (BLOCK * dil)
            for pr in range(2):
                for kind in range(3):
                    slab_ref[((g - 1) * 3 + kind) * 2 + pr, pl.ds(start, BLOCK, stride=dil), :] = res[pr][kind]
            return carry

        lax.fori_loop(0, dil * nj, body, 0)

    def final(j, carry):
        res = unit(0, 0, j)
        r0 = pl.multiple_of(j * BLOCK, BLOCK)
        for pr in range(2):
            os_, ms_, ls_ = [res[pr][0]], [res[pr][1]], [res[pr][2]]
            for g in range(1, ng):
                base = (g - 1) * 3
                os_.append(slab_ref[(base + _SLAB_O) * 2 + pr, pl.ds(r0, BLOCK), :])
                ms_.append(slab_ref[(base + _SLAB_M) * 2 + pr, pl.ds(r0, BLOCK), :])
                ls_.append(slab_ref[(base + _SLAB_L) * 2 + pr, pl.ds(r0, BLOCK), :])
            mx = functools.reduce(jnp.maximum, ms_)
            num = 0.0
            den = 0.0
            for o, m, l in zip(os_, ms_, ls_):
                a = jnp.exp(m - mx)
                num = num + a * o
                den = den + a * l
            o_ref[0, pl.ds(r0, BLOCK), pr * LANES:(pr + 1) * LANES] = (num / den).astype(o_ref.dtype)
        return carry

    lax.fori_loop(0, ts // BLOCK, final, 0)


def _attn_b(qkv, band):
    b = qkv[0][0].shape[0]
    s = qkv[0][0].shape[2]
    ts = TS_B
    nt_tiles = s // ts
    args, in_specs, kbuf_shapes, vbuf_shapes = [], [], [], []
    for (q, k, v), (_, dil) in zip(qkv, B_GROUPS):
        nt = ts // dil
        nh = nt // B_SIDE
        last = s // dil // B_SIDE - 1
        cur = pl.BlockSpec((1, dil, nt, B_GW), lambda i, j: (i, 0, j, 0))
        prev = pl.BlockSpec((1, dil, B_SIDE, B_GW), lambda i, j, nh=nh: (i, 0, jnp.maximum(j * nh - 1, 0), 0))
        nxt = pl.BlockSpec((1, dil, B_SIDE, B_GW),
                           lambda i, j, nh=nh, last=last: (i, 0, jnp.minimum((j + 1) * nh, last), 0))
        args += [q, k, k, k, v, v, v]
        in_specs += [cur, prev, cur, nxt, prev, cur, nxt]
        kbuf_shapes.append(pltpu.VMEM((dil, nt + 2 * B_SIDE, B_GW), _BF16))
        vbuf_shapes.append(pltpu.VMEM((dil, nt + 2 * B_SIDE, B_GW), _BF16))
    n_slabs = (len(B_GROUPS) - 1) * 3 * 2
    return pl.pallas_call(
        _attn_b_kernel,
        out_shape=jax.ShapeDtypeStruct((b, s, B_GW), _BF16),
        grid=(b, nt_tiles),
        in_specs=in_specs + [_const_spec(band.shape)],
        out_specs=pl.BlockSpec((1, ts, B_GW), lambda i, j: (i, j, 0)),
        scratch_shapes=kbuf_shapes + vbuf_shapes + [pltpu.VMEM((n_slabs, ts, LANES), _F32)],
        compiler_params=pltpu.CompilerParams(
            dimension_semantics=("arbitrary", "arbitrary"), vmem_limit_bytes=VMEM_LIMIT),
        name="attn_b",
    )(*args, band)


def _layer_norm(v, g, b):
    mu = jnp.mean(v, axis=-1, keepdims=True)
    d = v - mu
    var = jnp.mean(d * d, axis=-1, keepdims=True)
    return d * lax.rsqrt(var + LN_EPS) * g + b


def _merge_kernel(x_ref, oa_ref, ob_ref, wg_ref, bg_ref, wa_ref, wb_ref, wo_ref, g_ref, b_ref, h_ref):
    x = x_ref[...]
    xb = x.astype(_BF16)
    gates = jax.nn.sigmoid(jnp.dot(xb, wg_ref[...], preferred_element_type=_F32) + bg_ref[...])
    pa = jnp.dot(oa_ref[...], wa_ref[...], preferred_element_type=_F32)
    pb = jnp.dot(ob_ref[...], wb_ref[...], preferred_element_type=_F32)
    mixed = gates[:, :D_MODEL] * pa + gates[:, D_MODEL:] * pb
    y = jnp.dot(mixed.astype(_BF16), wo_ref[...], preferred_element_type=_F32)
    h_ref[...] = _layer_norm(DN_ALPHA * x + y, g_ref[...], b_ref[...])


def _merge(x2, oa2, ob2, wg, bg, wa, wb, wo, g1, b1):
    n = x2.shape[0]
    tm = TM_DENSE
    row = lambda w: pl.BlockSpec((tm, w), lambda i: (i, 0))
    consts = [wg, bg, wa, wb, wo, g1, b1]
    return pl.pallas_call(
        _merge_kernel,
        out_shape=jax.ShapeDtypeStruct((n, D_MODEL), _F32),
        grid=(n // tm,),
        in_specs=[row(D_MODEL), row(A_Q_W), row(B_GW)] + [_const_spec(c.shape) for c in consts],
        out_specs=row(D_MODEL),
        compiler_params=pltpu.CompilerParams(
            dimension_semantics=("arbitrary",), vmem_limit_bytes=VMEM_LIMIT),
        name="merge",
    )(x2, oa2, ob2, *consts)


def _ffn_kernel(h_ref, wg_ref, wu_ref, wd_ref, g_ref, b_ref, o_ref):
    h = h_ref[...]
    hb = h.astype(_BF16)
    gt = jnp.dot(hb, wg_ref[...], preferred_element_type=_F32)
    up = jnp.dot(hb, wu_ref[...], preferred_element_type=_F32)
    act = (jax.nn.silu(gt) * up).astype(_BF16)
    f = jnp.dot(act, wd_ref[...], preferred_element_type=_F32)
    o_ref[...] = _layer_norm(DN_ALPHA * h + f, g_ref[...], b_ref[...])


def _ffn(h2, wg, wu, wd, g2, b2):
    n = h2.shape[0]
    tm = TM_DENSE
    row = pl.BlockSpec((tm, D_MODEL), lambda i: (i, 0))
    consts = [wg, wu, wd, g2, b2]
    return pl.pallas_call(
        _ffn_kernel,
        out_shape=jax.ShapeDtypeStruct((n, D_MODEL), _F32),
        grid=(n // tm,),
        in_specs=[row] + [_const_spec(c.shape) for c in consts],
        out_specs=row,
        compiler_params=pltpu.CompilerParams(
            dimension_semantics=("arbitrary",), vmem_limit_bytes=VMEM_LIMIT),
        name="ffn",
    )(h2, *consts)


def _rope_tables(s):
    half = ROT_DIM // 2
    inv = ROPE_THETA ** (-jnp.arange(half, dtype=_F32) / half)
    ang = jnp.arange(s).astype(_F32)[:, None] * inv[None, :]
    cos, sin = jnp.cos(ang), jnp.sin(ang)
    ones = jnp.ones((s, HEAD_DIM - ROT_DIM), _F32)
    zeros = jnp.zeros((s, HEAD_DIM - half), _F32)
    cos_h = jnp.concatenate([cos, cos, ones], axis=1)
    sa_h = jnp.concatenate([-sin, zeros], axis=1)
    sb_h = jnp.concatenate([jnp.zeros((s, half), _F32), sin, jnp.zeros((s, HEAD_DIM - ROT_DIM), _F32)], axis=1)
    two = lambda a: jnp.concatenate([a, a], axis=1)
    return two(cos_h), two(sa_h), two(sb_h)


def _band_bias(nq, nk, lo, hi):
    rel = np.arange(nk)[None, :] - np.arange(nq)[:, None]
    return jnp.asarray(np.where((rel >= lo) & (rel <= hi), 0.0, NEG).astype(np.float32))


def kernel(x, w_in, a_sink, w_gate, b_gate, w_br_a, w_br_b, w_out, ln1_g, ln1_b,
           w_ff_gate, w_ff_up, w_ff_down, ln2_g, ln2_b):
    b, s, d = x.shape
    assert d == D_MODEL and s % TS_B == 0 and s % TQ_A == 0 and s % TM_PROJ == 0
    assert w_in.shape[0] == DEPTH == 1
    cos_t, sa_t, sb_t = _rope_tables(s)
    band_a = _band_bias(BLOCK, 3 * BLOCK, 0, 2 * A_WINDOW)
    band_b = _band_bias(BLOCK, BLOCK + 2 * B_SIDE, 0, 2 * B_SIDE)
    bf = lambda w: w[0].astype(_BF16)
    row = lambda v: v[0].astype(_F32)[None, :]

    outs = _proj(x, bf(w_in), cos_t, sa_t, sb_t)
    qa, ka2, va2 = outs[:3]
    qkv = [tuple(outs[3 + 3 * g:6 + 3 * g]) for g in range(len(B_GROUPS))]
    o_a = _attn_a(qa, ka2, va2, a_sink[0].astype(_F32), band_a)
    o_b = _attn_b(qkv, band_b)

    n = b * s
    h1 = _merge(x.reshape(n, d), o_a.reshape(n, A_Q_W), o_b.reshape(n, B_GW),
                bf(w_gate), row(b_gate), bf(w_br_a), bf(w_br_b), bf(w_out), row(ln1_g), row(ln1_b))
    out = _ffn(h1, bf(w_ff_gate), bf(w_ff_up), bf(w_ff_down), row(ln2_g), row(ln2_b))
    return out.reshape(b, s, d)
```

```python
import functools

import numpy as np
import jax
import jax.numpy as jnp
from jax import lax
from jax.experimental import pallas as pl
from jax.experimental.pallas import tpu as pltpu

D_MODEL = 1024
HEAD_DIM = 64
ROT_DIM = HEAD_DIM // 4
ROPE_THETA = 500000.0
BLOCK = 128
A_Q_HEADS = 16
A_KV_HEADS = 4
A_WINDOW = 128
B_GROUPS = ((128, 1), (512, 4), (2048, 16))
B_HEADS_PER_GROUP = 4
B_SIDE = 64
A_Q_W = A_Q_HEADS * HEAD_DIM
A_KV_W = A_KV_HEADS * HEAD_DIM
B_GW = B_HEADS_PER_GROUP * HEAD_DIM
B_W = len(B_GROUPS) * B_GW
OFF_KA = A_Q_W
OFF_VA = OFF_KA + A_KV_W
OFF_QB = OFF_VA + A_KV_W
OFF_KB = OFF_QB + B_W
OFF_VB = OFF_KB + B_W
DEPTH = 1
DN_ALPHA = (2 * DEPTH) ** 0.25
LN_EPS = 1e-5
NEG = -1e30
LOG2E = 1.4426950408889634
LANES = 128
VMEM_LIMIT = 56 * 1024 * 1024

TM_PROJ = 512
TQ_A = 1024
TS_B = 2048
TM_DENSE = 512
UNITS_B = 4

_BF16 = jnp.bfloat16
_F32 = jnp.float32
_CONTRACT_LAST = (((1,), (1,)), ((), ()))


def _const_spec(shape):
    nd = len(shape)
    return pl.BlockSpec(shape, lambda *_: (0,) * nd, pipeline_mode=pl.Buffered(1))


def _lane_lt64(rows):
    return lax.broadcasted_iota(jnp.int32, (rows, LANES), 1) < HEAD_DIM


def _edge_variant(tile, blk, n_tiles, n_blks):
    first = jnp.logical_and(tile == 0, blk == 0)
    last = jnp.logical_and(tile == n_tiles - 1, blk == n_blks - 1)
    return first.astype(jnp.int32) + 2 * last.astype(jnp.int32)


def _rope(t, cos, sa, sb):
    return t * cos + pltpu.roll(t, LANES - ROT_DIM // 2, 1) * sa + pltpu.roll(t, ROT_DIM // 2, 1) * sb


def _proj_kernel(x_ref, w_ref, cos_ref, sa_ref, sb_ref,
                 qa_ref, ka_ref, va_ref,
                 q0_ref, k0_ref, v0_ref, q1_ref, k1_ref, v1_ref, q2_ref, k2_ref, v2_ref,
                 slab_ref):
    tm = x_ref.shape[1]
    xb = x_ref[0].astype(_BF16)
    cos = cos_ref[...]
    sa = sa_ref[...]
    sb = sb_ref[...]
    scale = HEAD_DIM ** -0.5 * LOG2E
    lt64 = _lane_lt64(tm)

    def mm(c0, width=2 * LANES):
        return jnp.dot(xb, w_ref[:, c0:c0 + width], preferred_element_type=_F32)

    for c in range(A_Q_W // (2 * LANES)):
        r = mm(c * 2 * LANES)
        for h in range(2):
            t = _rope(r[:, h * LANES:(h + 1) * LANES], cos, sa, sb) * scale
            qa_ref[0, :, c * 2 * LANES + h * LANES:c * 2 * LANES + (h + 1) * LANES] = t.astype(_BF16)

    for off, out_ref, rot in ((OFF_KA, ka_ref, True), (OFF_VA, va_ref, False)):
        r = mm(off)
        for h in range(2):
            t = r[:, h * LANES:(h + 1) * LANES]
            if rot:
                t = _rope(t, cos, sa, sb)
            sw = pltpu.roll(t, HEAD_DIM, 1)
            out_ref[0, :, (2 * h) * LANES:(2 * h + 1) * LANES] = jnp.where(lt64, t, sw).astype(_BF16)
            out_ref[0, :, (2 * h + 1) * LANES:(2 * h + 2) * LANES] = jnp.where(lt64, sw, t).astype(_BF16)

    outs = ((q0_ref, k0_ref, v0_ref), (q1_ref, k1_ref, v1_ref), (q2_ref, k2_ref, v2_ref))
    for g, (_, dil) in enumerate(B_GROUPS):
        for kind, off in enumerate((OFF_QB, OFF_KB, OFF_VB)):
            out_ref = outs[g][kind]
            r = mm(off + g * B_GW)
            for h in range(2):
                t = r[:, h * LANES:(h + 1) * LANES]
                if kind < 2:
                    t = _rope(t, cos, sa, sb)
                if kind == 0:
                    t = t * scale
                if dil == 1:
                    out_ref[0, 0, :, h * LANES:(h + 1) * LANES] = t.astype(_BF16)
                else:
                    slab_ref[h] = t
            if dil > 1:
                rows = tm // dil
                for h in range(2):
                    for p in range(dil):
                        v = slab_ref[h, pl.ds(p, rows, stride=dil), :]
                        out_ref[0, p, :, h * LANES:(h + 1) * LANES] = v.astype(_BF16)


def _proj(x, w_in_bf, cos_t, sa_t, sb_t):
    b, s, _ = x.shape
    tm = TM_PROJ
    grid = (s // tm, b)
    tok = lambda i, j: (j, i, 0)
    tab = pl.BlockSpec((tm, LANES), lambda i, j: (i, 0))
    out_shape = [jax.ShapeDtypeStruct((b, s, A_Q_W), _BF16),
                 jax.ShapeDtypeStruct((b, s, 2 * A_KV_W), _BF16),
                 jax.ShapeDtypeStruct((b, s, 2 * A_KV_W), _BF16)]
    out_specs = [pl.BlockSpec((1, tm, A_Q_W), tok),
                 pl.BlockSpec((1, tm, 2 * A_KV_W), tok),
                 pl.BlockSpec((1, tm, 2 * A_KV_W), tok)]
    for _, dil in B_GROUPS:
        for _ in range(3):
            out_shape.append(jax.ShapeDtypeStruct((b, dil, s // dil, B_GW), _BF16))
            out_specs.append(pl.BlockSpec((1, dil, tm // dil, B_GW), lambda i, j: (j, 0, i, 0)))
    return pl.pallas_call(
        _proj_kernel,
        out_shape=out_shape,
        grid=grid,
        in_specs=[pl.BlockSpec((1, tm, D_MODEL), tok), _const_spec(w_in_bf.shape), tab, tab, tab],
        out_specs=out_specs,
        scratch_shapes=[pltpu.VMEM((2, tm, LANES), _F32)],
        compiler_params=pltpu.CompilerParams(
            dimension_semantics=("arbitrary", "arbitrary"), vmem_limit_bytes=VMEM_LIMIT),
        name="proj",
    )(x, w_in_bf, cos_t, sa_t, sb_t)


def _attn_a_kernel(sink_ref, q_ref, kp_ref, kc_ref, kn_ref, vp_ref, vc_ref, vn_ref, bandt_ref, eye_ref,
                   o_ref, kbuf, vbuf):
    tq = q_ref.shape[1]
    t = pl.program_id(1)
    nblk = tq // BLOCK
    kbuf[0:BLOCK] = kp_ref[0]
    kbuf[BLOCK:BLOCK + tq] = kc_ref[0]
    kbuf[BLOCK + tq:2 * BLOCK + tq] = kn_ref[0]
    vbuf[0:BLOCK] = vp_ref[0]
    vbuf[BLOCK:BLOCK + tq] = vc_ref[0]
    vbuf[BLOCK + tq:2 * BLOCK + tq] = vn_ref[0]
    nkeys = 3 * BLOCK
    grp = A_Q_HEADS // A_KV_HEADS
    lt64_q = _lane_lt64(BLOCK)
    lt64_k = _lane_lt64(nkeys)
    zq = jnp.zeros((BLOCK, LANES), _BF16)
    zk = jnp.zeros((nkeys, LANES), _BF16)
    eye = eye_ref[...]

    def block(j, carry):
        r0 = pl.multiple_of(j * BLOCK, BLOCK)
        bandt = bandt_ref[_edge_variant(t, j, pl.num_programs(1), nblk)]
        scs = []
        for k in range(A_KV_HEADS):
            kd = kbuf[pl.ds(r0, nkeys), k * LANES:(k + 1) * LANES]
            qs = []
            for pr in range(2):
                qp = q_ref[0, pl.ds(r0, BLOCK), (2 * k + pr) * LANES:(2 * k + pr + 1) * LANES]
                qs.append(jnp.where(lt64_q, qp, zq))
                qs.append(jnp.where(lt64_q, zq, qp))
            lhs = jnp.concatenate([jnp.concatenate(qs, axis=0), eye], axis=1)
            rhs = jnp.concatenate([kd, bandt], axis=1)
            scs.append(lax.dot_general(lhs, rhs, _CONTRACT_LAST, preferred_element_type=_F32))
        sms = []
        for k in range(A_KV_HEADS):
            ps, ls = [], []
            for i in range(grp):
                si = scs[k][i * BLOCK:(i + 1) * BLOCK]
                sk = sink_ref[grp * k + i] * LOG2E
                m = jnp.maximum(jnp.max(si, axis=1, keepdims=True), sk)
                e = jnp.exp2(si - m)
                ls.append(jnp.sum(e, axis=1, keepdims=True) + jnp.exp2(sk - m))
                ps.append(e.astype(_BF16))
            pst = jnp.concatenate([jnp.concatenate([ps[0], ps[1]], axis=1),
                                   jnp.concatenate([ps[2], ps[3]], axis=1)], axis=0)
            sms.append((pst, ls))
        for k in range(A_KV_HEADS):
            pst, ls = sms[k]
            vd = vbuf[pl.ds(r0, nkeys), k * LANES:(k + 1) * LANES]
            vbd = jnp.concatenate([jnp.where(lt64_k, vd, zk), jnp.where(lt64_k, zk, vd)], axis=0)
            o = jnp.dot(pst, vbd, preferred_element_type=_F32)
            for pr in range(2):
                den = jnp.where(lt64_q, ls[2 * pr], ls[2 * pr + 1])
                res = o[pr * BLOCK:(pr + 1) * BLOCK] / den
                o_ref[0, pl.ds(r0, BLOCK), (2 * k + pr) * LANES:(2 * k + pr + 1) * LANES] = res.astype(o_ref.dtype)
        return carry

    lax.fori_loop(0, nblk, block, 0)


def _attn_a(qa, ka2, va2, sink, bandt, eye):
    b, s, _ = qa.shape
    tq = TQ_A
    nb = tq // BLOCK
    last = s // BLOCK - 1
    cur = lambda i, j: (i, j, 0)
    prev = lambda i, j: (i, jnp.maximum(j * nb - 1, 0), 0)
    nxt = lambda i, j: (i, jnp.minimum((j + 1) * nb, last), 0)
    kvw = ka2.shape[-1]
    kv_specs = [pl.BlockSpec((1, BLOCK, kvw), prev), pl.BlockSpec((1, tq, kvw), cur),
                pl.BlockSpec((1, BLOCK, kvw), nxt)]
    return pl.pallas_call(
        _attn_a_kernel,
        out_shape=jax.ShapeDtypeStruct((b, s, A_Q_W), _BF16),
        grid=(b, s // tq),
        in_specs=[pl.BlockSpec(memory_space=pltpu.SMEM),
                  pl.BlockSpec((1, tq, A_Q_W), cur)] + kv_specs + kv_specs
                 + [_const_spec(bandt.shape), _const_spec(eye.shape)],
        out_specs=pl.BlockSpec((1, tq, A_Q_W), cur),
        scratch_shapes=[pltpu.VMEM((tq + 2 * BLOCK, kvw), _BF16),
                        pltpu.VMEM((tq + 2 * BLOCK, kvw), _BF16)],
        compiler_params=pltpu.CompilerParams(
            dimension_semantics=("arbitrary", "arbitrary"), vmem_limit_bytes=VMEM_LIMIT),
        name="attn_a",
    )(sink, qa, ka2, ka2, ka2, va2, va2, va2, bandt, eye)


_SLAB_O, _SLAB_M, _SLAB_L = 0, 1, 2


def _attn_b_kernel(*refs):
    ng = len(B_GROUPS)
    in_refs = refs[:7 * ng]
    bandt_ref, eye_ref = refs[7 * ng:7 * ng + 2]
    o_ref = refs[7 * ng + 2]
    kbufs = refs[7 * ng + 3:7 * ng + 3 + ng]
    vbufs = refs[7 * ng + 3 + ng:7 * ng + 3 + 2 * ng]
    slab_ref = refs[7 * ng + 3 + 2 * ng]
    ts = o_ref.shape[1]
    t = pl.program_id(1)
    nkeys = BLOCK + 2 * B_SIDE
    lt64_q = _lane_lt64(BLOCK)
    lt64_k = _lane_lt64(nkeys)
    zq = jnp.zeros((BLOCK, LANES), _BF16)
    zk = jnp.zeros((nkeys, LANES), _BF16)
    eye = eye_ref[...]

    for g in range(ng):
        _, kp_ref, kc_ref, kn_ref, vp_ref, vc_ref, vn_ref = in_refs[7 * g:7 * g + 7]
        nt = kc_ref.shape[2]
        for buf, pr, cr, nr in ((kbufs[g], kp_ref, kc_ref, kn_ref), (vbufs[g], vp_ref, vc_ref, vn_ref)):
            buf[:, 0:B_SIDE] = pr[0]
            buf[:, B_SIDE:B_SIDE + nt] = cr[0]
            buf[:, B_SIDE + nt:2 * B_SIDE + nt] = nr[0]

    def units(g, pjs):
        q_ref = in_refs[7 * g]
        nj = ts // B_GROUPS[g][1] // BLOCK
        scs, vals = [], []
        for p, j in pjs:
            r0 = pl.multiple_of(j * BLOCK, BLOCK)
            bandt = bandt_ref[_edge_variant(t, j, pl.num_programs(1), nj)]
            for pr in range(2):
                cols = slice(pr * LANES, (pr + 1) * LANES)
                qp = q_ref[0, p, pl.ds(r0, BLOCK), cols]
                kp = kbufs[g][p, pl.ds(r0, nkeys), cols]
                qst = jnp.concatenate([jnp.where(lt64_q, qp, zq), jnp.where(lt64_q, zq, qp)], axis=0)
                lhs = jnp.concatenate([qst, eye], axis=1)
                rhs = jnp.concatenate([kp, bandt], axis=1)
                scs.append(lax.dot_general(lhs, rhs, _CONTRACT_LAST, preferred_element_type=_F32))
                vals.append((p, r0, cols))
        sms = []
        for sc in scs:
            m = jnp.max(sc, axis=1, keepdims=True)
            e = jnp.exp2(sc - m)
            l = jnp.sum(e, axis=1, keepdims=True)
            eb = e.astype(_BF16)
            pcat = jnp.concatenate([eb[:BLOCK], eb[BLOCK:]], axis=1)
            mb = jnp.where(lt64_q, m[:BLOCK], m[BLOCK:])
            lb = jnp.where(lt64_q, l[:BLOCK], l[BLOCK:])
            sms.append((pcat, mb, lb))
        res = []
        for (pcat, mb, lb), (p, r0, cols) in zip(sms, vals):
            vp = vbufs[g][p, pl.ds(r0, nkeys), cols]
            vbd = jnp.concatenate([jnp.where(lt64_k, vp, zk), jnp.where(lt64_k, zk, vp)], axis=0)
            o = jnp.dot(pcat, vbd, preferred_element_type=_F32)
            res.append((o, mb, lb))
        return [res[2 * i:2 * i + 2] for i in range(len(pjs))]

    for g in range(1, ng):
        dil = B_GROUPS[g][1]
        nj = ts // dil // BLOCK

        def body(it, carry, g=g, dil=dil, nj=nj):
            pjs = [((it * UNITS_B + i) // nj, (it * UNITS_B + i) % nj) for i in range(UNITS_B)]
            for (p, j), res in zip(pjs, units(g, pjs)):
                start = p + j * (BLOCK * dil)
                for pr in range(2):
                    for kind in range(3):
                        slab_ref[((g - 1) * 3 + kind) * 2 + pr, pl.ds(start, BLOCK, stride=dil), :] = res[pr][kind]
            return carry

        lax.fori_loop(0, dil * nj // UNITS_B, body, 0)

    def final(it, carry):
        pjs = [(0, it * UNITS_B + i) for i in range(UNITS_B)]
        for (_, j), res in zip(pjs, units(0, pjs)):
            r0 = pl.multiple_of(j * BLOCK, BLOCK)
            for pr in range(2):
                os_, ms_, ls_ = [res[pr][0]], [res[pr][1]], [res[pr][2]]
                for g in range(1, ng):
                    base = (g - 1) * 3
                    os_.append(slab_ref[(base + _SLAB_O) * 2 + pr, pl.ds(r0, BLOCK), :])
                    ms_.append(slab_ref[(base + _SLAB_M) * 2 + pr, pl.ds(r0, BLOCK), :])
                    ls_.append(slab_ref[(base + _SLAB_L) * 2 + pr, pl.ds(r0, BLOCK), :])
                mx = functools.reduce(jnp.maximum, ms_)
                num = 0.0
                den = 0.0
                for o, m, l in zip(os_, ms_, ls_):
                    a = jnp.exp2(m - mx)
                    num = num + a * o
                    den = den + a * l
                o_ref[0, pl.ds(r0, BLOCK), pr * LANES:(pr + 1) * LANES] = (num / den).astype(o_ref.dtype)
        return carry

    lax.fori_loop(0, ts // BLOCK // UNITS_B, final, 0)


def _attn_b(qkv, bandt, eye):
    b = qkv[0][0].shape[0]
    s = qkv[0][0].shape[2]
    ts = TS_B
    nt_tiles = s // ts
    args, in_specs, kbuf_shapes, vbuf_shapes = [], [], [], []
    for (q, k, v), (_, dil) in zip(qkv, B_GROUPS):
        nt = ts // dil
        nh = nt // B_SIDE
        last = s // dil // B_SIDE - 1
        cur = pl.BlockSpec((1, dil, nt, B_GW), lambda i, j: (i, 0, j, 0))
        prev = pl.BlockSpec((1, dil, B_SIDE, B_GW), lambda i, j, nh=nh: (i, 0, jnp.maximum(j * nh - 1, 0), 0))
        nxt = pl.BlockSpec((1, dil, B_SIDE, B_GW),
                           lambda i, j, nh=nh, last=last: (i, 0, jnp.minimum((j + 1) * nh, last), 0))
        args += [q, k, k, k, v, v, v]
        in_specs += [cur, prev, cur, nxt, prev, cur, nxt]
        kbuf_shapes.append(pltpu.VMEM((dil, nt + 2 * B_SIDE, B_GW), _BF16))
        vbuf_shapes.append(pltpu.VMEM((dil, nt + 2 * B_SIDE, B_GW), _BF16))
    n_slabs = (len(B_GROUPS) - 1) * 3 * 2
    return pl.pallas_call(
        _attn_b_kernel,
        out_shape=jax.ShapeDtypeStruct((b, s, B_GW), _BF16),
        grid=(b, nt_tiles),
        in_specs=in_specs + [_const_spec(bandt.shape), _const_spec(eye.shape)],
        out_specs=pl.BlockSpec((1, ts, B_GW), lambda i, j: (i, j, 0)),
        scratch_shapes=kbuf_shapes + vbuf_shapes + [pltpu.VMEM((n_slabs, ts, LANES), _F32)],
        compiler_params=pltpu.CompilerParams(
            dimension_semantics=("arbitrary", "arbitrary"), vmem_limit_bytes=VMEM_LIMIT),
        name="attn_b",
    )(*args, bandt, eye)


def _layer_norm(v, g, b):
    mu = jnp.mean(v, axis=-1, keepdims=True)
    d = v - mu
    var = jnp.mean(d * d, axis=-1, keepdims=True)
    return d * lax.rsqrt(var + LN_EPS) * g + b


def _merge_kernel(x_ref, oa_ref, ob_ref, wg_ref, bg_ref, wa_ref, wb_ref, wo_ref, g_ref, b_ref, h_ref):
    x = x_ref[...]
    xb = x.astype(_BF16)
    gates = jax.nn.sigmoid(jnp.dot(xb, wg_ref[...], preferred_element_type=_F32) + bg_ref[...])
    pa = jnp.dot(oa_ref[...], wa_ref[...], preferred_element_type=_F32)
    pb = jnp.dot(ob_ref[...], wb_ref[...], preferred_element_type=_F32)
    mixed = gates[:, :D_MODEL] * pa + gates[:, D_MODEL:] * pb
    y = jnp.dot(mixed.astype(_BF16), wo_ref[...], preferred_element_type=_F32)
    h_ref[...] = _layer_norm(DN_ALPHA * x + y, g_ref[...], b_ref[...])


def _merge(x2, oa2, ob2, wg, bg, wa, wb, wo, g1, b1):
    n = x2.shape[0]
    tm = TM_DENSE
    row = lambda w: pl.BlockSpec((tm, w), lambda i: (i, 0))
    consts = [wg, bg, wa, wb, wo, g1, b1]
    return pl.pallas_call(
        _merge_kernel,
        out_shape=jax.ShapeDtypeStruct((n, D_MODEL), _F32),
        grid=(n // tm,),
        in_specs=[row(D_MODEL), row(A_Q_W), row(B_GW)] + [_const_spec(c.shape) for c in consts],
        out_specs=row(D_MODEL),
        compiler_params=pltpu.CompilerParams(
            dimension_semantics=("arbitrary",), vmem_limit_bytes=VMEM_LIMIT),
        name="merge",
    )(x2, oa2, ob2, *consts)


def _ffn_kernel(h_ref, wg_ref, wu_ref, wd_ref, g_ref, b_ref, o_ref):
    h = h_ref[...]
    hb = h.astype(_BF16)
    gt = jnp.dot(hb, wg_ref[...], preferred_element_type=_F32)
    up = jnp.dot(hb, wu_ref[...], preferred_element_type=_F32)
    act = (jax.nn.silu(gt) * up).astype(_BF16)
    f = jnp.dot(act, wd_ref[...], preferred_element_type=_F32)
    o_ref[...] = _layer_norm(DN_ALPHA * h + f, g_ref[...], b_ref[...])


def _ffn(h2, wg, wu, wd, g2, b2):
    n = h2.shape[0]
    tm = TM_DENSE
    row = pl.BlockSpec((tm, D_MODEL), lambda i: (i, 0))
    consts = [wg, wu, wd, g2, b2]
    return pl.pallas_call(
        _ffn_kernel,
        out_shape=jax.ShapeDtypeStruct((n, D_MODEL), _F32),
        grid=(n // tm,),
        in_specs=[row] + [_const_spec(c.shape) for c in consts],
        out_specs=row,
        compiler_params=pltpu.CompilerParams(
            dimension_semantics=("arbitrary",), vmem_limit_bytes=VMEM_LIMIT),
        name="ffn",
    )(h2, *consts)


def _rope_tables(s):
    half = ROT_DIM // 2
    inv = ROPE_THETA ** (-jnp.arange(half, dtype=_F32) / half)
    ang = jnp.arange(s).astype(_F32)[:, None] * inv[None, :]
    cos, sin = jnp.cos(ang), jnp.sin(ang)
    ones = jnp.ones((s, HEAD_DIM - ROT_DIM), _F32)
    zeros = jnp.zeros((s, HEAD_DIM - half), _F32)
    cos_h = jnp.concatenate([cos, cos, ones], axis=1)
    sa_h = jnp.concatenate([-sin, zeros], axis=1)
    sb_h = jnp.concatenate([jnp.zeros((s, half), _F32), sin, jnp.zeros((s, HEAD_DIM - ROT_DIM), _F32)], axis=1)
    two = lambda a: jnp.concatenate([a, a], axis=1)
    return two(cos_h), two(sa_h), two(sb_h)


def _band_t(nk, halo, width):
    key = np.arange(nk)[:, None]
    rel = key - np.arange(BLOCK)[None, :]
    band = (rel >= 0) & (rel <= width)
    lead, trail = key >= halo, key < nk - halo
    variants = [band, band & lead, band & trail, band & lead & trail]
    return jnp.asarray(np.where(np.stack(variants), 0.0, NEG).astype(np.float32)).astype(_BF16)


def _stacked_eye(heads):
    return jnp.asarray(np.tile(np.eye(BLOCK, dtype=np.float32), (heads, 1))).astype(_BF16)


def kernel(x, w_in, a_sink, w_gate, b_gate, w_br_a, w_br_b, w_out, ln1_g, ln1_b,
           w_ff_gate, w_ff_up, w_ff_down, ln2_g, ln2_b):
    b, s, d = x.shape
    assert d == D_MODEL and s % TS_B == 0 and s % TQ_A == 0 and s % TM_PROJ == 0
    assert w_in.shape[0] == DEPTH == 1
    cos_t, sa_t, sb_t = _rope_tables(s)
    bf = lambda w: w[0].astype(_BF16)
    row = lambda v: v[0].astype(_F32)[None, :]

    outs = _proj(x, bf(w_in), cos_t, sa_t, sb_t)
    qa, ka2, va2 = outs[:3]
    qkv = [tuple(outs[3 + 3 * g:6 + 3 * g]) for g in range(len(B_GROUPS))]
    o_a = _attn_a(qa, ka2, va2, a_sink[0].astype(_F32),
                  _band_t(3 * BLOCK, BLOCK, 2 * A_WINDOW), _stacked_eye(A_Q_HEADS // A_KV_HEADS))
    o_b = _attn_b(qkv, _band_t(BLOCK + 2 * B_SIDE, B_SIDE, 2 * B_SIDE), _stacked_eye(2))

    n = b * s
    h1 = _merge(x.reshape(n, d), o_a.reshape(n, A_Q_W), o_b.reshape(n, B_GW),
                bf(w_gate), row(b_gate), bf(w_br_a), bf(w_br_b), bf(w_out), row(ln1_g), row(ln1_b))
    out = _ffn(h1, bf(w_ff_gate), bf(w_ff_up), bf(w_ff_down), row(ln2_g), row(ln2_b))
    return out.reshape(b, s, d)
```

```python
import functools

import numpy as np
import jax
import jax.numpy as jnp
from jax import lax
from jax.experimental import pallas as pl
from jax.experimental.pallas import tpu as pltpu

D_MODEL = 1024
HEAD_DIM = 64
ROT_DIM = HEAD_DIM // 4
ROPE_THETA = 500000.0
BLOCK = 128
A_Q_HEADS = 16
A_KV_HEADS = 4
A_WINDOW = 128
B_GROUPS = ((128, 1), (512, 4), (2048, 16))
B_HEADS_PER_GROUP = 4
B_SIDE = 64
A_Q_W = A_Q_HEADS * HEAD_DIM
A_KV_W = A_KV_HEADS * HEAD_DIM
B_GW = B_HEADS_PER_GROUP * HEAD_DIM
B_W = len(B_GROUPS) * B_GW
OFF_KA = A_Q_W
OFF_VA = OFF_KA + A_KV_W
OFF_QB = OFF_VA + A_KV_W
OFF_KB = OFF_QB + B_W
OFF_VB = OFF_KB + B_W
DEPTH = 1
DN_ALPHA = (2 * DEPTH) ** 0.25
LN_EPS = 1e-5
NEG = -1e30
LOG2E = 1.4426950408889634
LANES = 128
VMEM_LIMIT = 56 * 1024 * 1024

TM_PROJ = 512
TQ_A = 1024
TS_B = 2048
TM_DENSE = 512
UNITS_B = 4
UNITS_FINAL = 4
PHASE_MAJOR_DIL = 16
SLAB_PITCH = BLOCK + 8

_BF16 = jnp.bfloat16
_F32 = jnp.float32
_CONTRACT_LAST = (((1,), (1,)), ((), ()))


def _const_spec(shape):
    nd = len(shape)
    return pl.BlockSpec(shape, lambda *_: (0,) * nd, pipeline_mode=pl.Buffered(1))


def _lane_lt64(rows):
    return lax.broadcasted_iota(jnp.int32, (rows, LANES), 1) < HEAD_DIM


def _edge_variant(tile, blk, n_tiles, n_blks):
    first = jnp.logical_and(tile == 0, blk == 0)
    last = jnp.logical_and(tile == n_tiles - 1, blk == n_blks - 1)
    return first.astype(jnp.int32) + 2 * last.astype(jnp.int32)


def _rope(t, cos, sa, sb):
    return t * cos + pltpu.roll(t, LANES - ROT_DIM // 2, 1) * sa + pltpu.roll(t, ROT_DIM // 2, 1) * sb


def _proj_kernel(x_ref, w_ref, cos_ref, sa_ref, sb_ref,
                 qa_ref, ka_ref, va_ref,
                 q0_ref, k0_ref, v0_ref, q1_ref, k1_ref, v1_ref, q2_ref, k2_ref, v2_ref,
                 slab_ref):
    tm = x_ref.shape[1]
    xb = x_ref[0].astype(_BF16)
    cos = cos_ref[...]
    sa = sa_ref[...]
    sb = sb_ref[...]
    scale = HEAD_DIM ** -0.5 * LOG2E
    lt64 = _lane_lt64(tm)

    def mm(c0, width=2 * LANES):
        return jnp.dot(xb, w_ref[:, c0:c0 + width], preferred_element_type=_F32)

    for c in range(A_Q_W // (2 * LANES)):
        r = mm(c * 2 * LANES)
        for h in range(2):
            t = _rope(r[:, h * LANES:(h + 1) * LANES], cos, sa, sb) * scale
            qa_ref[0, :, c * 2 * LANES + h * LANES:c * 2 * LANES + (h + 1) * LANES] = t.astype(_BF16)

    for off, out_ref, rot in ((OFF_KA, ka_ref, True), (OFF_VA, va_ref, False)):
        r = mm(off)
        for h in range(2):
            t = r[:, h * LANES:(h + 1) * LANES]
            if rot:
                t = _rope(t, cos, sa, sb)
            sw = pltpu.roll(t, HEAD_DIM, 1)
            out_ref[0, :, (2 * h) * LANES:(2 * h + 1) * LANES] = jnp.where(lt64, t, sw).astype(_BF16)
            out_ref[0, :, (2 * h + 1) * LANES:(2 * h + 2) * LANES] = jnp.where(lt64, sw, t).astype(_BF16)

    outs = ((q0_ref, k0_ref, v0_ref), (q1_ref, k1_ref, v1_ref), (q2_ref, k2_ref, v2_ref))
    for g, (_, dil) in enumerate(B_GROUPS):
        for kind, off in enumerate((OFF_QB, OFF_KB, OFF_VB)):
            out_ref = outs[g][kind]
            r = mm(off + g * B_GW)
            for h in range(2):
                t = r[:, h * LANES:(h + 1) * LANES]
                if kind < 2:
                    t = _rope(t, cos, sa, sb)
                if kind == 0:
                    t = t * scale
                if dil == 1:
                    out_ref[0, 0, :, h * LANES:(h + 1) * LANES] = t.astype(_BF16)
                else:
                    slab_ref[h] = t
            if dil > 1:
                rows = tm // dil
                for h in range(2):
                    for p in range(dil):
                        v = slab_ref[h, pl.ds(p, rows, stride=dil), :]
                        out_ref[0, p, :, h * LANES:(h + 1) * LANES] = v.astype(_BF16)


def _proj(x, w_in_bf, cos_t, sa_t, sb_t):
    b, s, _ = x.shape
    tm = TM_PROJ
    grid = (s // tm, b)
    tok = lambda i, j: (j, i, 0)
    tab = pl.BlockSpec((tm, LANES), lambda i, j: (i, 0))
    out_shape = [jax.ShapeDtypeStruct((b, s, A_Q_W), _BF16),
                 jax.ShapeDtypeStruct((b, s, 2 * A_KV_W), _BF16),
                 jax.ShapeDtypeStruct((b, s, 2 * A_KV_W), _BF16)]
    out_specs = [pl.BlockSpec((1, tm, A_Q_W), tok),
                 pl.BlockSpec((1, tm, 2 * A_KV_W), tok),
                 pl.BlockSpec((1, tm, 2 * A_KV_W), tok)]
    for _, dil in B_GROUPS:
        for _ in range(3):
            out_shape.append(jax.ShapeDtypeStruct((b, dil, s // dil, B_GW), _BF16))
            out_specs.append(pl.BlockSpec((1, dil, tm // dil, B_GW), lambda i, j: (j, 0, i, 0)))
    return pl.pallas_call(
        _proj_kernel,
        out_shape=out_shape,
        grid=grid,
        in_specs=[pl.BlockSpec((1, tm, D_MODEL), tok), _const_spec(w_in_bf.shape), tab, tab, tab],
        out_specs=out_specs,
        scratch_shapes=[pltpu.VMEM((2, tm, LANES), _F32)],
        compiler_params=pltpu.CompilerParams(
            dimension_semantics=("arbitrary", "arbitrary"), vmem_limit_bytes=VMEM_LIMIT),
        name="proj",
    )(x, w_in_bf, cos_t, sa_t, sb_t)


def _attn_a_kernel(sink_ref, q_ref, kp_ref, kc_ref, kn_ref, vp_ref, vc_ref, vn_ref, bandt_ref, eye_ref,
                   ones_ref, o_ref, kbuf, vbuf):
    tq = q_ref.shape[1]
    t = pl.program_id(1)
    nblk = tq // BLOCK
    kbuf[0:BLOCK] = kp_ref[0]
    kbuf[BLOCK:BLOCK + tq] = kc_ref[0]
    kbuf[BLOCK + tq:2 * BLOCK + tq] = kn_ref[0]
    vbuf[0:BLOCK] = vp_ref[0]
    vbuf[BLOCK:BLOCK + tq] = vc_ref[0]
    vbuf[BLOCK + tq:2 * BLOCK + tq] = vn_ref[0]
    nkeys = 3 * BLOCK
    grp = A_Q_HEADS // A_KV_HEADS
    lt64_q = _lane_lt64(BLOCK)
    lt64_k = _lane_lt64(nkeys)
    zq = jnp.zeros((BLOCK, LANES), _BF16)
    zk = jnp.zeros((nkeys, LANES), _BF16)
    eye = eye_ref[...]
    ones_bd = ones_ref[...]

    def block(j, carry):
        r0 = pl.multiple_of(j * BLOCK, BLOCK)
        bandt = bandt_ref[_edge_variant(t, j, pl.num_programs(1), nblk)]
        scs = []
        for k in range(A_KV_HEADS):
            kd = kbuf[pl.ds(r0, nkeys), k * LANES:(k + 1) * LANES]
            qs = []
            for pr in range(2):
                qp = q_ref[0, pl.ds(r0, BLOCK), (2 * k + pr) * LANES:(2 * k + pr + 1) * LANES]
                qs.append(jnp.where(lt64_q, qp, zq))
                qs.append(jnp.where(lt64_q, zq, qp))
            lhs = jnp.concatenate([jnp.concatenate(qs, axis=0), eye], axis=1)
            rhs = jnp.concatenate([kd, bandt], axis=1)
            scs.append(lax.dot_general(lhs, rhs, _CONTRACT_LAST, preferred_element_type=_F32))
        sms = []
        for k in range(A_KV_HEADS):
            ps, sinks = [], []
            for i in range(grp):
                si = scs[k][i * BLOCK:(i + 1) * BLOCK]
                sk = sink_ref[grp * k + i] * LOG2E
                m = jnp.maximum(jnp.max(si, axis=1, keepdims=True), sk)
                ps.append(jnp.exp2((si - m).astype(_BF16)))
                sinks.append(jnp.exp2(sk - m))
            pst = jnp.concatenate([jnp.concatenate([ps[0], ps[1]], axis=1),
                                   jnp.concatenate([ps[2], ps[3]], axis=1)], axis=0)
            sms.append((pst, sinks))
        for k in range(A_KV_HEADS):
            pst, sinks = sms[k]
            vd = vbuf[pl.ds(r0, nkeys), k * LANES:(k + 1) * LANES]
            vbd = jnp.concatenate([jnp.where(lt64_k, vd, zk), jnp.where(lt64_k, zk, vd)], axis=0)
            ol = jnp.dot(pst, jnp.concatenate([vbd, ones_bd], axis=1), preferred_element_type=_F32)
            for pr in range(2):
                rows = slice(pr * BLOCK, (pr + 1) * BLOCK)
                den = ol[rows, LANES:] + jnp.where(lt64_q, sinks[2 * pr], sinks[2 * pr + 1])
                res = ol[rows, :LANES] / den
                o_ref[0, pl.ds(r0, BLOCK), (2 * k + pr) * LANES:(2 * k + pr + 1) * LANES] = res.astype(o_ref.dtype)
        return carry

    lax.fori_loop(0, nblk, block, 0)


def _attn_a(qa, ka2, va2, sink, bandt, eye, ones_bd):
    b, s, _ = qa.shape
    tq = TQ_A
    nb = tq // BLOCK
    last = s // BLOCK - 1
    cur = lambda i, j: (i, j, 0)
    prev = lambda i, j: (i, jnp.maximum(j * nb - 1, 0), 0)
    nxt = lambda i, j: (i, jnp.minimum((j + 1) * nb, last), 0)
    kvw = ka2.shape[-1]
    kv_specs = [pl.BlockSpec((1, BLOCK, kvw), prev), pl.BlockSpec((1, tq, kvw), cur),
                pl.BlockSpec((1, BLOCK, kvw), nxt)]
    return pl.pallas_call(
        _attn_a_kernel,
        out_shape=jax.ShapeDtypeStruct((b, s, A_Q_W), _BF16),
        grid=(b, s // tq),
        in_specs=[pl.BlockSpec(memory_space=pltpu.SMEM),
                  pl.BlockSpec((1, tq, A_Q_W), cur)] + kv_specs + kv_specs
                 + [_const_spec(bandt.shape), _const_spec(eye.shape), _const_spec(ones_bd.shape)],
        out_specs=pl.BlockSpec((1, tq, A_Q_W), cur),
        scratch_shapes=[pltpu.VMEM((tq + 2 * BLOCK, kvw), _BF16),
                        pltpu.VMEM((tq + 2 * BLOCK, kvw), _BF16)],
        compiler_params=pltpu.CompilerParams(
            dimension_semantics=("arbitrary", "arbitrary"), vmem_limit_bytes=VMEM_LIMIT),
        name="attn_a",
    )(sink, qa, ka2, ka2, ka2, va2, va2, va2, bandt, eye, ones_bd)


_SLAB_O, _SLAB_M, _SLAB_L = 0, 1, 2


def _attn_b_kernel(*refs):
    ng = len(B_GROUPS)
    in_refs = refs[:7 * ng]
    bandt_ref, eye_ref, ones_ref, o_ref = refs[7 * ng:7 * ng + 4]
    scratch = refs[7 * ng + 4:]
    kbufs, vbufs = scratch[:ng], scratch[ng:2 * ng]
    slab_refs = (None,) + tuple(scratch[2 * ng:])
    ts = o_ref.shape[1]
    t = pl.program_id(1)
    nkeys = BLOCK + 2 * B_SIDE
    lt64_q = _lane_lt64(BLOCK)
    lt64_k = _lane_lt64(nkeys)
    zq = jnp.zeros((BLOCK, LANES), _BF16)
    zk = jnp.zeros((nkeys, LANES), _BF16)
    eye = eye_ref[...]
    ones_bd = ones_ref[...]

    for g in range(ng):
        _, kp_ref, kc_ref, kn_ref, vp_ref, vc_ref, vn_ref = in_refs[7 * g:7 * g + 7]
        nt = kc_ref.shape[2]
        for buf, pr, cr, nr in ((kbufs[g], kp_ref, kc_ref, kn_ref), (vbufs[g], vp_ref, vc_ref, vn_ref)):
            buf[:, 0:B_SIDE] = pr[0]
            buf[:, B_SIDE:B_SIDE + nt] = cr[0]
            buf[:, B_SIDE + nt:2 * B_SIDE + nt] = nr[0]

    def units(g, pjs):
        q_ref = in_refs[7 * g]
        nj = ts // B_GROUPS[g][1] // BLOCK
        scs, vals = [], []
        for p, j in pjs:
            r0 = pl.multiple_of(j * BLOCK, BLOCK)
            bandt = bandt_ref[_edge_variant(t, j, pl.num_programs(1), nj)]
            for pr in range(2):
                cols = slice(pr * LANES, (pr + 1) * LANES)
                qp = q_ref[0, p, pl.ds(r0, BLOCK), cols]
                kp = kbufs[g][p, pl.ds(r0, nkeys), cols]
                qst = jnp.concatenate([jnp.where(lt64_q, qp, zq), jnp.where(lt64_q, zq, qp)], axis=0)
                lhs = jnp.concatenate([qst, eye], axis=1)
                rhs = jnp.concatenate([kp, bandt], axis=1)
                scs.append(lax.dot_general(lhs, rhs, _CONTRACT_LAST, preferred_element_type=_F32))
                vals.append((p, r0, cols))
        sms = []
        for sc in scs:
            m = jnp.max(sc, axis=1, keepdims=True)
            eb = jnp.exp2((sc - m).astype(_BF16))
            pcat = jnp.concatenate([eb[:BLOCK], eb[BLOCK:]], axis=1)
            sms.append((pcat, jnp.where(lt64_q, m[:BLOCK], m[BLOCK:])))
        res = []
        for (pcat, mb), (p, r0, cols) in zip(sms, vals):
            vp = vbufs[g][p, pl.ds(r0, nkeys), cols]
            vbd = jnp.concatenate([jnp.where(lt64_k, vp, zk), jnp.where(lt64_k, zk, vp)], axis=0)
            ol = jnp.dot(pcat, jnp.concatenate([vbd, ones_bd], axis=1), preferred_element_type=_F32)
            res.append((ol[:, :LANES], mb, ol[:, LANES:]))
        return [res[2 * i:2 * i + 2] for i in range(len(pjs))]

    for g in range(1, ng):
        dil = B_GROUPS[g][1]
        nj = ts // dil // BLOCK

        def body(it, carry, g=g, dil=dil, nj=nj):
            pjs = [((it * UNITS_B + i) // nj, (it * UNITS_B + i) % nj) for i in range(UNITS_B)]
            for (p, j), res in zip(pjs, units(g, pjs)):
                if dil == PHASE_MAJOR_DIL:
                    rows = pl.ds(pl.multiple_of(p * SLAB_PITCH, 8), BLOCK)
                else:
                    rows = pl.ds(p + j * (BLOCK * dil), BLOCK, stride=dil)
                for pr in range(2):
                    for kind in range(3):
                        slab_refs[g][kind * 2 + pr, rows, :] = res[pr][kind]
            return carry

        lax.fori_loop(0, dil * nj // UNITS_B, body, 0)

    def slab_block(g, idx, j):
        dil = B_GROUPS[g][1]
        if dil != PHASE_MAJOR_DIL:
            return slab_refs[g][idx, pl.ds(pl.multiple_of(j * BLOCK, BLOCK), BLOCK), :]
        per = BLOCK // dil
        pieces = [slab_refs[g][idx, pl.ds((8 * (r % 2)) * SLAB_PITCH + j * per + r // 2, 8, stride=SLAB_PITCH), :]
                  for r in range(BLOCK // 8)]
        return jnp.concatenate(pieces, axis=0)

    def final(it, carry):
        pjs = [(0, it * UNITS_FINAL + i) for i in range(UNITS_FINAL)]
        for (_, j), res in zip(pjs, units(0, pjs)):
            r0 = pl.multiple_of(j * BLOCK, BLOCK)
            for pr in range(2):
                os_, ms_, ls_ = [res[pr][0]], [res[pr][1]], [res[pr][2]]
                for g in range(1, ng):
                    os_.append(slab_block(g, _SLAB_O * 2 + pr, j))
                    ms_.append(slab_block(g, _SLAB_M * 2 + pr, j))
                    ls_.append(slab_block(g, _SLAB_L * 2 + pr, j))
                mx = functools.reduce(jnp.maximum, ms_)
                num = 0.0
                den = 0.0
                for o, m, l in zip(os_, ms_, ls_):
                    a = jnp.exp2(m - mx)
                    num = num + a * o
                    den = den + a * l
                o_ref[0, pl.ds(r0, BLOCK), pr * LANES:(pr + 1) * LANES] = (num / den).astype(o_ref.dtype)
        return carry

    lax.fori_loop(0, ts // BLOCK // UNITS_FINAL, final, 0)


def _attn_b(qkv, bandt, eye, ones_bd):
    b = qkv[0][0].shape[0]
    s = qkv[0][0].shape[2]
    ts = TS_B
    nt_tiles = s // ts
    args, in_specs, kbuf_shapes, vbuf_shapes = [], [], [], []
    for (q, k, v), (_, dil) in zip(qkv, B_GROUPS):
        nt = ts // dil
        nh = nt // B_SIDE
        last = s // dil // B_SIDE - 1
        cur = pl.BlockSpec((1, dil, nt, B_GW), lambda i, j: (i, 0, j, 0))
        prev = pl.BlockSpec((1, dil, B_SIDE, B_GW), lambda i, j, nh=nh: (i, 0, jnp.maximum(j * nh - 1, 0), 0))
        nxt = pl.BlockSpec((1, dil, B_SIDE, B_GW),
                           lambda i, j, nh=nh, last=last: (i, 0, jnp.minimum((j + 1) * nh, last), 0))
        args += [q, k, k, k, v, v, v]
        in_specs += [cur, prev, cur, nxt, prev, cur, nxt]
        kbuf_shapes.append(pltpu.VMEM((dil, nt + 2 * B_SIDE, B_GW), _BF16))
        vbuf_shapes.append(pltpu.VMEM((dil, nt + 2 * B_SIDE, B_GW), _BF16))
    slab_shapes = []
    for _, dil in B_GROUPS[1:]:
        rows = dil * SLAB_PITCH if dil == PHASE_MAJOR_DIL else ts
        slab_shapes.append(pltpu.VMEM((3 * 2, rows, LANES), _F32))
    return pl.pallas_call(
        _attn_b_kernel,
        out_shape=jax.ShapeDtypeStruct((b, s, B_GW), _BF16),
        grid=(b, nt_tiles),
        in_specs=in_specs + [_const_spec(bandt.shape), _const_spec(eye.shape), _const_spec(ones_bd.shape)],
        out_specs=pl.BlockSpec((1, ts, B_GW), lambda i, j: (i, j, 0)),
        scratch_shapes=kbuf_shapes + vbuf_shapes + slab_shapes,
        compiler_params=pltpu.CompilerParams(
            dimension_semantics=("arbitrary", "arbitrary"), vmem_limit_bytes=VMEM_LIMIT),
        name="attn_b",
    )(*args, bandt, eye, ones_bd)


def _layer_norm(v, g, b):
    mu = jnp.mean(v, axis=-1, keepdims=True)
    d = v - mu
    var = jnp.mean(d * d, axis=-1, keepdims=True)
    return d * lax.rsqrt(var + LN_EPS) * g + b


def _merge_kernel(x_ref, oa_ref, ob_ref, wg_ref, bg_ref, wa_ref, wb_ref, wo_ref, g_ref, b_ref, h_ref):
    x = x_ref[...]
    xb = x.astype(_BF16)
    gates = jax.nn.sigmoid(jnp.dot(xb, wg_ref[...], preferred_element_type=_F32) + bg_ref[...])
    pa = jnp.dot(oa_ref[...], wa_ref[...], preferred_element_type=_F32)
    pb = jnp.dot(ob_ref[...], wb_ref[...], preferred_element_type=_F32)
    mixed = gates[:, :D_MODEL] * pa + gates[:, D_MODEL:] * pb
    y = jnp.dot(mixed.astype(_BF16), wo_ref[...], preferred_element_type=_F32)
    h_ref[...] = _layer_norm(DN_ALPHA * x + y, g_ref[...], b_ref[...])


def _merge(x2, oa2, ob2, wg, bg, wa, wb, wo, g1, b1):
    n = x2.shape[0]
    tm = TM_DENSE
    row = lambda w: pl.BlockSpec((tm, w), lambda i: (i, 0))
    consts = [wg, bg, wa, wb, wo, g1, b1]
    return pl.pallas_call(
        _merge_kernel,
        out_shape=jax.ShapeDtypeStruct((n, D_MODEL), _F32),
        grid=(n // tm,),
        in_specs=[row(D_MODEL), row(A_Q_W), row(B_GW)] + [_const_spec(c.shape) for c in consts],
        out_specs=row(D_MODEL),
        compiler_params=pltpu.CompilerParams(
            dimension_semantics=("arbitrary",), vmem_limit_bytes=VMEM_LIMIT),
        name="merge",
    )(x2, oa2, ob2, *consts)


def _ffn_kernel(h_ref, wg_ref, wu_ref, wd_ref, g_ref, b_ref, o_ref):
    h = h_ref[...]
    hb = h.astype(_BF16)
    gt = jnp.dot(hb, wg_ref[...], preferred_element_type=_F32)
    up = jnp.dot(hb, wu_ref[...], preferred_element_type=_F32)
    act = (jax.nn.silu(gt) * up).astype(_BF16)
    f = jnp.dot(act, wd_ref[...], preferred_element_type=_F32)
    o_ref[...] = _layer_norm(DN_ALPHA * h + f, g_ref[...], b_ref[...])


def _ffn(h2, wg, wu, wd, g2, b2):
    n = h2.shape[0]
    tm = TM_DENSE
    row = pl.BlockSpec((tm, D_MODEL), lambda i: (i, 0))
    consts = [wg, wu, wd, g2, b2]
    return pl.pallas_call(
        _ffn_kernel,
        out_shape=jax.ShapeDtypeStruct((n, D_MODEL), _F32),
        grid=(n // tm,),
        in_specs=[row] + [_const_spec(c.shape) for c in consts],
        out_specs=row,
        compiler_params=pltpu.CompilerParams(
            dimension_semantics=("arbitrary",), vmem_limit_bytes=VMEM_LIMIT),
        name="ffn",
    )(h2, *consts)


def _rope_tables(s):
    half = ROT_DIM // 2
    inv = np.float32(ROPE_THETA) ** (-np.arange(half, dtype=np.float32) / np.float32(half))
    ang = np.arange(s).astype(np.float32)[:, None] * inv[None, :]
    cos, sin = np.cos(ang).astype(np.float32), np.sin(ang).astype(np.float32)
    zeros = lambda w: np.zeros((s, w), np.float32)
    cos_h = np.concatenate([cos, cos, np.ones((s, HEAD_DIM - ROT_DIM), np.float32)], axis=1)
    sa_h = np.concatenate([-sin, zeros(HEAD_DIM - half)], axis=1)
    sb_h = np.concatenate([zeros(half), sin, zeros(HEAD_DIM - ROT_DIM)], axis=1)
    two = lambda a: jnp.asarray(np.concatenate([a, a], axis=1))
    return two(cos_h), two(sa_h), two(sb_h)


def _band_t(nk, halo, width):
    key = np.arange(nk)[:, None]
    rel = key - np.arange(BLOCK)[None, :]
    band = (rel >= 0) & (rel <= width)
    lead, trail = key >= halo, key < nk - halo
    variants = [band, band & lead, band & trail, band & lead & trail]
    return jnp.asarray(np.where(np.stack(variants), 0.0, NEG).astype(np.float32)).astype(_BF16)


def _ones_block_diag(nkeys):
    lo = np.broadcast_to(np.arange(LANES) < HEAD_DIM, (nkeys, LANES))
    return jnp.asarray(np.concatenate([lo, ~lo], axis=0).astype(np.float32)).astype(_BF16)


def _stacked_eye(heads):
    return jnp.asarray(np.tile(np.eye(BLOCK, dtype=np.float32), (heads, 1))).astype(_BF16)


def kernel(x, w_in, a_sink, w_gate, b_gate, w_br_a, w_br_b, w_out, ln1_g, ln1_b,
           w_ff_gate, w_ff_up, w_ff_down, ln2_g, ln2_b):
    b, s, d = x.shape
    assert d == D_MODEL and s % TS_B == 0 and s % TQ_A == 0 and s % TM_PROJ == 0
    assert w_in.shape[0] == DEPTH == 1
    cos_t, sa_t, sb_t = _rope_tables(s)
    bf = lambda w: w[0].astype(_BF16)
    row = lambda v: v[0].astype(_F32)[None, :]

    outs = _proj(x, bf(w_in), cos_t, sa_t, sb_t)
    qa, ka2, va2 = outs[:3]
    qkv = [tuple(outs[3 + 3 * g:6 + 3 * g]) for g in range(len(B_GROUPS))]
    o_a = _attn_a(qa, ka2, va2, a_sink[0].astype(_F32),
                  _band_t(3 * BLOCK, BLOCK, 2 * A_WINDOW), _stacked_eye(A_Q_HEADS // A_KV_HEADS),
                  _ones_block_diag(3 * BLOCK))
    o_b = _attn_b(qkv, _band_t(BLOCK + 2 * B_SIDE, B_SIDE, 2 * B_SIDE), _stacked_eye(2),
                  _ones_block_diag(BLOCK + 2 * B_SIDE))

    n = b * s
    h1 = _merge(x.reshape(n, d), o_a.reshape(n, A_Q_W), o_b.reshape(n, B_GW),
                bf(w_gate), row(b_gate), bf(w_br_a), bf(w_br_b), bf(w_out), row(ln1_g), row(ln1_b))
    out = _ffn(h1, bf(w_ff_gate), bf(w_ff_up), bf(w_ff_down), row(ln2_g), row(ln2_b))
    return out.reshape(b, s, d)
```

```python
import functools

import numpy as np
import jax
import jax.numpy as jnp
from jax import lax
from jax.experimental import pallas as pl
from jax.experimental.pallas import tpu as pltpu

D_MODEL = 1024
HEAD_DIM = 64
ROT_DIM = HEAD_DIM // 4
ROPE_THETA = 500000.0
BLOCK = 128
A_Q_HEADS = 16
A_KV_HEADS = 4
A_WINDOW = 128
B_GROUPS = ((128, 1), (512, 4), (2048, 16))
B_HEADS_PER_GROUP = 4
B_SIDE = 64
A_Q_W = A_Q_HEADS * HEAD_DIM
A_KV_W = A_KV_HEADS * HEAD_DIM
B_GW = B_HEADS_PER_GROUP * HEAD_DIM
B_W = len(B_GROUPS) * B_GW
OFF_KA = A_Q_W
OFF_VA = OFF_KA + A_KV_W
OFF_QB = OFF_VA + A_KV_W
OFF_KB = OFF_QB + B_W
OFF_VB = OFF_KB + B_W
DEPTH = 1
DN_ALPHA = (2 * DEPTH) ** 0.25
LN_EPS = 1e-5
NEG = -1e30
LOG2E = 1.4426950408889634
LANES = 128
VMEM_LIMIT = 56 * 1024 * 1024

TM_PROJ = 512
TQ_A = 2048
TS_B = 2048
TM_DENSE = 512
KV_PER_STAGE = 4
UNITS_B = 4
UNITS_FINAL = 4
PHASE_MAJOR_DIL = 16
SLAB_PITCH = BLOCK + 8

_BF16 = jnp.bfloat16
_F32 = jnp.float32
_CONTRACT_LAST = (((1,), (1,)), ((), ()))


def _const_spec(shape):
    nd = len(shape)
    return pl.BlockSpec(shape, lambda *_: (0,) * nd, pipeline_mode=pl.Buffered(1))


def _lane_lt64(rows):
    return lax.broadcasted_iota(jnp.int32, (rows, LANES), 1) < HEAD_DIM


def _edge_variant(tile, blk, n_tiles, n_blks):
    first = jnp.logical_and(tile == 0, blk == 0)
    last = jnp.logical_and(tile == n_tiles - 1, blk == n_blks - 1)
    return first.astype(jnp.int32) + 2 * last.astype(jnp.int32)


def _rope(t, cos, sa, sb):
    return t * cos + pltpu.roll(t, LANES - ROT_DIM // 2, 1) * sa + pltpu.roll(t, ROT_DIM // 2, 1) * sb


def _proj_kernel(x_ref, w_ref, cos_ref, sa_ref, sb_ref,
                 qa_ref, ka_ref, va_ref,
                 q0_ref, k0_ref, v0_ref, q1_ref, k1_ref, v1_ref, q2_ref, k2_ref, v2_ref,
                 slab_ref):
    tm = x_ref.shape[1]
    xb = x_ref[0].astype(_BF16)
    cos = cos_ref[...]
    sa = sa_ref[...]
    sb = sb_ref[...]
    scale = HEAD_DIM ** -0.5 * LOG2E
    lt64 = _lane_lt64(tm)

    def mm(c0, width=2 * LANES):
        return jnp.dot(xb, w_ref[:, c0:c0 + width], preferred_element_type=_F32)

    for c in range(A_Q_W // (2 * LANES)):
        r = mm(c * 2 * LANES)
        for h in range(2):
            t = _rope(r[:, h * LANES:(h + 1) * LANES], cos, sa, sb) * scale
            qa_ref[0, :, c * 2 * LANES + h * LANES:c * 2 * LANES + (h + 1) * LANES] = t.astype(_BF16)

    for off, out_ref, rot in ((OFF_KA, ka_ref, True), (OFF_VA, va_ref, False)):
        r = mm(off)
        for h in range(2):
            t = r[:, h * LANES:(h + 1) * LANES]
            if rot:
                t = _rope(t, cos, sa, sb)
            sw = pltpu.roll(t, HEAD_DIM, 1)
            out_ref[0, :, (2 * h) * LANES:(2 * h + 1) * LANES] = jnp.where(lt64, t, sw).astype(_BF16)
            out_ref[0, :, (2 * h + 1) * LANES:(2 * h + 2) * LANES] = jnp.where(lt64, sw, t).astype(_BF16)

    outs = ((q0_ref, k0_ref, v0_ref), (q1_ref, k1_ref, v1_ref), (q2_ref, k2_ref, v2_ref))
    for g, (_, dil) in enumerate(B_GROUPS):
        for kind, off in enumerate((OFF_QB, OFF_KB, OFF_VB)):
            out_ref = outs[g][kind]
            r = mm(off + g * B_GW)
            for h in range(2):
                t = r[:, h * LANES:(h + 1) * LANES]
                if kind < 2:
                    t = _rope(t, cos, sa, sb)
                if kind == 0:
                    t = t * scale
                if dil == 1:
                    out_ref[0, 0, :, h * LANES:(h + 1) * LANES] = t.astype(_BF16)
                else:
                    slab_ref[h] = t
            if dil > 1:
                rows = tm // dil
                for h in range(2):
                    for p in range(dil):
                        v = slab_ref[h, pl.ds(p, rows, stride=dil), :]
                        out_ref[0, p, :, h * LANES:(h + 1) * LANES] = v.astype(_BF16)


def _proj(x, w_in_bf, cos_t, sa_t, sb_t):
    b, s, _ = x.shape
    tm = TM_PROJ
    grid = (s // tm, b)
    tok = lambda i, j: (j, i, 0)
    tab = pl.BlockSpec((tm, LANES), lambda i, j: (i, 0))
    out_shape = [jax.ShapeDtypeStruct((b, s, A_Q_W), _BF16),
                 jax.ShapeDtypeStruct((b, s, 2 * A_KV_W), _BF16),
                 jax.ShapeDtypeStruct((b, s, 2 * A_KV_W), _BF16)]
    out_specs = [pl.BlockSpec((1, tm, A_Q_W), tok),
                 pl.BlockSpec((1, tm, 2 * A_KV_W), tok),
                 pl.BlockSpec((1, tm, 2 * A_KV_W), tok)]
    for _, dil in B_GROUPS:
        for _ in range(3):
            out_shape.append(jax.ShapeDtypeStruct((b, dil, s // dil, B_GW), _BF16))
            out_specs.append(pl.BlockSpec((1, dil, tm // dil, B_GW), lambda i, j: (j, 0, i, 0)))
    return pl.pallas_call(
        _proj_kernel,
        out_shape=out_shape,
        grid=grid,
        in_specs=[pl.BlockSpec((1, tm, D_MODEL), tok), _const_spec(w_in_bf.shape), tab, tab, tab],
        out_specs=out_specs,
        scratch_shapes=[pltpu.VMEM((2, tm, LANES), _F32)],
        compiler_params=pltpu.CompilerParams(
            dimension_semantics=("arbitrary", "arbitrary"), vmem_limit_bytes=VMEM_LIMIT),
        name="proj",
    )(x, w_in_bf, cos_t, sa_t, sb_t)


def _attn_a_kernel(sink_ref, q_ref, kp_ref, kc_ref, kn_ref, vp_ref, vc_ref, vn_ref, bandt_ref, eye_ref,
                   ones_ref, o_ref, kbuf, vbuf):
    tq = q_ref.shape[1]
    t = pl.program_id(1)
    nblk = tq // BLOCK
    kbuf[0:BLOCK] = kp_ref[0]
    kbuf[BLOCK:BLOCK + tq] = kc_ref[0]
    kbuf[BLOCK + tq:2 * BLOCK + tq] = kn_ref[0]
    vbuf[0:BLOCK] = vp_ref[0]
    vbuf[BLOCK:BLOCK + tq] = vc_ref[0]
    vbuf[BLOCK + tq:2 * BLOCK + tq] = vn_ref[0]
    nkeys = 3 * BLOCK
    grp = A_Q_HEADS // A_KV_HEADS
    lt64_q = _lane_lt64(BLOCK)
    lt64_k = _lane_lt64(nkeys)
    zq = jnp.zeros((BLOCK, LANES), _BF16)
    zk = jnp.zeros((nkeys, LANES), _BF16)
    eye = eye_ref[...]
    ones_bd = ones_ref[...]

    def block(j, carry):
        r0 = pl.multiple_of(j * BLOCK, BLOCK)
        bandt = bandt_ref[_edge_variant(t, j, pl.num_programs(1), nblk)]
        for k0 in range(0, A_KV_HEADS, KV_PER_STAGE):
            ks = range(k0, k0 + KV_PER_STAGE)
            scs = {}
            for k in ks:
                kd = kbuf[pl.ds(r0, nkeys), k * LANES:(k + 1) * LANES]
                qs = []
                for pr in range(2):
                    qp = q_ref[0, pl.ds(r0, BLOCK), (2 * k + pr) * LANES:(2 * k + pr + 1) * LANES]
                    qs.append(jnp.where(lt64_q, qp, zq))
                    qs.append(jnp.where(lt64_q, zq, qp))
                lhs = jnp.concatenate([jnp.concatenate(qs, axis=0), eye], axis=1)
                rhs = jnp.concatenate([kd, bandt], axis=1)
                scs[k] = lax.dot_general(lhs, rhs, _CONTRACT_LAST, preferred_element_type=_F32)
            sms = {}
            for k in ks:
                ps, sinks = [], []
                for i in range(grp):
                    si = scs[k][i * BLOCK:(i + 1) * BLOCK]
                    sk = sink_ref[grp * k + i] * LOG2E
                    m = jnp.maximum(jnp.max(si, axis=1, keepdims=True), sk)
                    ps.append(jnp.exp2((si - m).astype(_BF16)))
                    sinks.append(jnp.exp2(sk - m))
                pst = jnp.concatenate([jnp.concatenate([ps[0], ps[1]], axis=1),
                                       jnp.concatenate([ps[2], ps[3]], axis=1)], axis=0)
                sms[k] = (pst, sinks)
            for k in ks:
                pst, sinks = sms[k]
                vd = vbuf[pl.ds(r0, nkeys), k * LANES:(k + 1) * LANES]
                vbd = jnp.concatenate([jnp.where(lt64_k, vd, zk), jnp.where(lt64_k, zk, vd)], axis=0)
                ol = jnp.dot(pst, jnp.concatenate([vbd, ones_bd], axis=1), preferred_element_type=_F32)
                for pr in range(2):
                    rows = slice(pr * BLOCK, (pr + 1) * BLOCK)
                    den = ol[rows, LANES:] + jnp.where(lt64_q, sinks[2 * pr], sinks[2 * pr + 1])
                    res = ol[rows, :LANES] / den
                    o_ref[0, pl.ds(r0, BLOCK), (2 * k + pr) * LANES:(2 * k + pr + 1) * LANES] = res.astype(o_ref.dtype)
        return carry

    lax.fori_loop(0, nblk, block, 0, unroll=8)


def _attn_a(qa, ka2, va2, sink, bandt, eye, ones_bd):
    b, s, _ = qa.shape
    tq = TQ_A
    nb = tq // BLOCK
    last = s // BLOCK - 1
    cur = lambda i, j: (i, j, 0)
    prev = lambda i, j: (i, jnp.maximum(j * nb - 1, 0), 0)
    nxt = lambda i, j: (i, jnp.minimum((j + 1) * nb, last), 0)
    kvw = ka2.shape[-1]
    kv_specs = [pl.BlockSpec((1, BLOCK, kvw), prev), pl.BlockSpec((1, tq, kvw), cur),
                pl.BlockSpec((1, BLOCK, kvw), nxt)]
    return pl.pallas_call(
        _attn_a_kernel,
        out_shape=jax.ShapeDtypeStruct((b, s, A_Q_W), _BF16),
        grid=(b, s // tq),
        in_specs=[pl.BlockSpec(memory_space=pltpu.SMEM),
                  pl.BlockSpec((1, tq, A_Q_W), cur)] + kv_specs + kv_specs
                 + [_const_spec(bandt.shape), _const_spec(eye.shape), _const_spec(ones_bd.shape)],
        out_specs=pl.BlockSpec((1, tq, A_Q_W), cur),
        scratch_shapes=[pltpu.VMEM((tq + 2 * BLOCK, kvw), _BF16),
                        pltpu.VMEM((tq + 2 * BLOCK, kvw), _BF16)],
        compiler_params=pltpu.CompilerParams(
            dimension_semantics=("arbitrary", "arbitrary"), vmem_limit_bytes=VMEM_LIMIT),
        name="attn_a",
    )(sink, qa, ka2, ka2, ka2, va2, va2, va2, bandt, eye, ones_bd)


_SLAB_O, _SLAB_M, _SLAB_L = 0, 1, 2


def _attn_b_kernel(*refs):
    ng = len(B_GROUPS)
    in_refs = refs[:7 * ng]
    bandt_ref, eye_ref, ones_ref, o_ref = refs[7 * ng:7 * ng + 4]
    scratch = refs[7 * ng + 4:]
    kbufs, vbufs = scratch[:ng], scratch[ng:2 * ng]
    slab_refs = (None,) + tuple(scratch[2 * ng:])
    ts = o_ref.shape[1]
    t = pl.program_id(1)
    nkeys = BLOCK + 2 * B_SIDE
    lt64_q = _lane_lt64(BLOCK)
    lt64_k = _lane_lt64(nkeys)
    zq = jnp.zeros((BLOCK, LANES), _BF16)
    zk = jnp.zeros((nkeys, LANES), _BF16)
    eye = eye_ref[...]
    ones_bd = ones_ref[...]

    for g in range(ng):
        _, kp_ref, kc_ref, kn_ref, vp_ref, vc_ref, vn_ref = in_refs[7 * g:7 * g + 7]
        nt = kc_ref.shape[2]
        for buf, pr, cr, nr in ((kbufs[g], kp_ref, kc_ref, kn_ref), (vbufs[g], vp_ref, vc_ref, vn_ref)):
            buf[:, 0:B_SIDE] = pr[0]
            buf[:, B_SIDE:B_SIDE + nt] = cr[0]
            buf[:, B_SIDE + nt:2 * B_SIDE + nt] = nr[0]

    def units(g, pjs):
        q_ref = in_refs[7 * g]
        nj = ts // B_GROUPS[g][1] // BLOCK
        scs, vals = [], []
        for p, j in pjs:
            r0 = pl.multiple_of(j * BLOCK, BLOCK)
            bandt = bandt_ref[_edge_variant(t, j, pl.num_programs(1), nj)]
            for pr in range(2):
                cols = slice(pr * LANES, (pr + 1) * LANES)
                qp = q_ref[0, p, pl.ds(r0, BLOCK), cols]
                kp = kbufs[g][p, pl.ds(r0, nkeys), cols]
                qst = jnp.concatenate([jnp.where(lt64_q, qp, zq), jnp.where(lt64_q, zq, qp)], axis=0)
                lhs = jnp.concatenate([qst, eye], axis=1)
                rhs = jnp.concatenate([kp, bandt], axis=1)
                scs.append(lax.dot_general(lhs, rhs, _CONTRACT_LAST, preferred_element_type=_F32))
                vals.append((p, r0, cols))
        sms = []
        for sc in scs:
            m = jnp.max(sc, axis=1, keepdims=True)
            eb = jnp.exp2((sc - m).astype(_BF16))
            pcat = jnp.concatenate([eb[:BLOCK], eb[BLOCK:]], axis=1)
            sms.append((pcat, jnp.where(lt64_q, m[:BLOCK], m[BLOCK:])))
        res = []
        for (pcat, mb), (p, r0, cols) in zip(sms, vals):
            vp = vbufs[g][p, pl.ds(r0, nkeys), cols]
            vbd = jnp.concatenate([jnp.where(lt64_k, vp, zk), jnp.where(lt64_k, zk, vp)], axis=0)
            ol = jnp.dot(pcat, jnp.concatenate([vbd, ones_bd], axis=1), preferred_element_type=_F32)
            res.append((ol[:, :LANES], mb, ol[:, LANES:]))
        return [res[2 * i:2 * i + 2] for i in range(len(pjs))]

    for g in range(1, ng):
        dil = B_GROUPS[g][1]
        nj = ts // dil // BLOCK

        def body(it, carry, g=g, dil=dil, nj=nj):
            pjs = [((it * UNITS_B + i) // nj, (it * UNITS_B + i) % nj) for i in range(UNITS_B)]
            for (p, j), res in zip(pjs, units(g, pjs)):
                if dil == PHASE_MAJOR_DIL:
                    rows = pl.ds(pl.multiple_of(p * SLAB_PITCH, 8), BLOCK)
                else:
                    rows = pl.ds(p + j * (BLOCK * dil), BLOCK, stride=dil)
                for pr in range(2):
                    for kind in range(3):
                        slab_refs[g][kind * 2 + pr, rows, :] = res[pr][kind]
            return carry

        lax.fori_loop(0, dil * nj // UNITS_B, body, 0, unroll=4)

    def slab_block(g, idx, j):
        dil = B_GROUPS[g][1]
        if dil != PHASE_MAJOR_DIL:
            return slab_refs[g][idx, pl.ds(pl.multiple_of(j * BLOCK, BLOCK), BLOCK), :]
        per = BLOCK // dil
        pieces = [slab_refs[g][idx, pl.ds((8 * (r % 2)) * SLAB_PITCH + j * per + r // 2, 8, stride=SLAB_PITCH), :]
                  for r in range(BLOCK // 8)]
        return jnp.concatenate(pieces, axis=0)

    def final(it, carry):
        pjs = [(0, it * UNITS_FINAL + i) for i in range(UNITS_FINAL)]
        for (_, j), res in zip(pjs, units(0, pjs)):
            r0 = pl.multiple_of(j * BLOCK, BLOCK)
            for pr in range(2):
                os_, ms_, ls_ = [res[pr][0]], [res[pr][1]], [res[pr][2]]
                for g in range(1, ng):
                    os_.append(slab_block(g, _SLAB_O * 2 + pr, j))
                    ms_.append(slab_block(g, _SLAB_M * 2 + pr, j))
                    ls_.append(slab_block(g, _SLAB_L * 2 + pr, j))
                mx = functools.reduce(jnp.maximum, ms_)
                num = 0.0
                den = 0.0
                for o, m, l in zip(os_, ms_, ls_):
                    a = jnp.exp2(m - mx)
                    num = num + a * o
                    den = den + a * l
                o_ref[0, pl.ds(r0, BLOCK), pr * LANES:(pr + 1) * LANES] = (num / den).astype(o_ref.dtype)
        return carry

    lax.fori_loop(0, ts // BLOCK // UNITS_FINAL, final, 0, unroll=4)


def _attn_b(qkv, bandt, eye, ones_bd):
    b = qkv[0][0].shape[0]
    s = qkv[0][0].shape[2]
    ts = TS_B
    nt_tiles = s // ts
    args, in_specs, kbuf_shapes, vbuf_shapes = [], [], [], []
    for (q, k, v), (_, dil) in zip(qkv, B_GROUPS):
        nt = ts // dil
        nh = nt // B_SIDE
        last = s // dil // B_SIDE - 1
        cur = pl.BlockSpec((1, dil, nt, B_GW), lambda i, j: (i, 0, j, 0))
        prev = pl.BlockSpec((1, dil, B_SIDE, B_GW), lambda i, j, nh=nh: (i, 0, jnp.maximum(j * nh - 1, 0), 0))
        nxt = pl.BlockSpec((1, dil, B_SIDE, B_GW),
                           lambda i, j, nh=nh, last=last: (i, 0, jnp.minimum((j + 1) * nh, last), 0))
        args += [q, k, k, k, v, v, v]
        in_specs += [cur, prev, cur, nxt, prev, cur, nxt]
        kbuf_shapes.append(pltpu.VMEM((dil, nt + 2 * B_SIDE, B_GW), _BF16))
        vbuf_shapes.append(pltpu.VMEM((dil, nt + 2 * B_SIDE, B_GW), _BF16))
    slab_shapes = []
    for _, dil in B_GROUPS[1:]:
        rows = dil * SLAB_PITCH if dil == PHASE_MAJOR_DIL else ts
        slab_shapes.append(pltpu.VMEM((3 * 2, rows, LANES), _F32))
    return pl.pallas_call(
        _attn_b_kernel,
        out_shape=jax.ShapeDtypeStruct((b, s, B_GW), _BF16),
        grid=(b, nt_tiles),
        in_specs=in_specs + [_const_spec(bandt.shape), _const_spec(eye.shape), _const_spec(ones_bd.shape)],
        out_specs=pl.BlockSpec((1, ts, B_GW), lambda i, j: (i, j, 0)),
        scratch_shapes=kbuf_shapes + vbuf_shapes + slab_shapes,
        compiler_params=pltpu.CompilerParams(
            dimension_semantics=("arbitrary", "arbitrary"), vmem_limit_bytes=VMEM_LIMIT),
        name="attn_b",
    )(*args, bandt, eye, ones_bd)


def _layer_norm(v, g, b):
    mu = jnp.mean(v, axis=-1, keepdims=True)
    d = v - mu
    var = jnp.mean(d * d, axis=-1, keepdims=True)
    return d * lax.rsqrt(var + LN_EPS) * g + b


def _merge_kernel(x_ref, oa_ref, ob_ref, wg_ref, bg_ref, wa_ref, wb_ref, wo_ref, g_ref, b_ref, h_ref):
    x = x_ref[...]
    xb = x.astype(_BF16)
    gates = jax.nn.sigmoid(jnp.dot(xb, wg_ref[...], preferred_element_type=_F32) + bg_ref[...])
    pa = jnp.dot(oa_ref[...], wa_ref[...], preferred_element_type=_F32)
    pb = jnp.dot(ob_ref[...], wb_ref[...], preferred_element_type=_F32)
    mixed = gates[:, :D_MODEL] * pa + gates[:, D_MODEL:] * pb
    y = jnp.dot(mixed.astype(_BF16), wo_ref[...], preferred_element_type=_F32)
    h_ref[...] = _layer_norm(DN_ALPHA * x + y, g_ref[...], b_ref[...])


def _merge(x2, oa2, ob2, wg, bg, wa, wb, wo, g1, b1):
    n = x2.shape[0]
    tm = TM_DENSE
    row = lambda w: pl.BlockSpec((tm, w), lambda i: (i, 0))
    consts = [wg, bg, wa, wb, wo, g1, b1]
    return pl.pallas_call(
        _merge_kernel,
        out_shape=jax.ShapeDtypeStruct((n, D_MODEL), _F32),
        grid=(n // tm,),
        in_specs=[row(D_MODEL), row(A_Q_W), row(B_GW)] + [_const_spec(c.shape) for c in consts],
        out_specs=row(D_MODEL),
        compiler_params=pltpu.CompilerParams(
            dimension_semantics=("arbitrary",), vmem_limit_bytes=VMEM_LIMIT),
        name="merge",
    )(x2, oa2, ob2, *consts)


def _ffn_kernel(h_ref, wg_ref, wu_ref, wd_ref, g_ref, b_ref, o_ref):
    h = h_ref[...]
    hb = h.astype(_BF16)
    gt = jnp.dot(hb, wg_ref[...], preferred_element_type=_F32)
    up = jnp.dot(hb, wu_ref[...], preferred_element_type=_F32)
    act = (jax.nn.silu(gt) * up).astype(_BF16)
    f = jnp.dot(act, wd_ref[...], preferred_element_type=_F32)
    o_ref[...] = _layer_norm(DN_ALPHA * h + f, g_ref[...], b_ref[...])


def _ffn(h2, wg, wu, wd, g2, b2):
    n = h2.shape[0]
    tm = TM_DENSE
    row = pl.BlockSpec((tm, D_MODEL), lambda i: (i, 0))
    consts = [wg, wu, wd, g2, b2]
    return pl.pallas_call(
        _ffn_kernel,
        out_shape=jax.ShapeDtypeStruct((n, D_MODEL), _F32),
        grid=(n // tm,),
        in_specs=[row] + [_const_spec(c.shape) for c in consts],
        out_specs=row,
        compiler_params=pltpu.CompilerParams(
            dimension_semantics=("arbitrary",), vmem_limit_bytes=VMEM_LIMIT),
        name="ffn",
    )(h2, *consts)


def _rope_tables(s):
    half = ROT_DIM // 2
    inv = np.float32(ROPE_THETA) ** (-np.arange(half, dtype=np.float32) / np.float32(half))
    ang = np.arange(s).astype(np.float32)[:, None] * inv[None, :]
    cos, sin = np.cos(ang).astype(np.float32), np.sin(ang).astype(np.float32)
    zeros = lambda w: np.zeros((s, w), np.float32)
    cos_h = np.concatenate([cos, cos, np.ones((s, HEAD_DIM - ROT_DIM), np.float32)], axis=1)
    sa_h = np.concatenate([-sin, zeros(HEAD_DIM - half)], axis=1)
    sb_h = np.concatenate([zeros(half), sin, zeros(HEAD_DIM - ROT_DIM)], axis=1)
    two = lambda a: jnp.asarray(np.concatenate([a, a], axis=1))
    return two(cos_h), two(sa_h), two(sb_h)


def _band_t(nk, halo, width):
    key = np.arange(nk)[:, None]
    rel = key - np.arange(BLOCK)[None, :]
    band = (rel >= 0) & (rel <= width)
    lead, trail = key >= halo, key < nk - halo
    variants = [band, band & lead, band & trail, band & lead & trail]
    return jnp.asarray(np.where(np.stack(variants), 0.0, NEG).astype(np.float32)).astype(_BF16)


def _ones_block_diag(nkeys):
    lo = np.broadcast_to(np.arange(LANES) < HEAD_DIM, (nkeys, LANES))
    return jnp.asarray(np.concatenate([lo, ~lo], axis=0).astype(np.float32)).astype(_BF16)


def _stacked_eye(heads):
    return jnp.asarray(np.tile(np.eye(BLOCK, dtype=np.float32), (heads, 1))).astype(_BF16)


def kernel(x, w_in, a_sink, w_gate, b_gate, w_br_a, w_br_b, w_out, ln1_g, ln1_b,
           w_ff_gate, w_ff_up, w_ff_down, ln2_g, ln2_b):
    b, s, d = x.shape
    assert d == D_MODEL and s % TS_B == 0 and s % TQ_A == 0 and s % TM_PROJ == 0
    assert w_in.shape[0] == DEPTH == 1
    cos_t, sa_t, sb_t = _rope_tables(s)
    bf = lambda w: w[0].astype(_BF16)
    row = lambda v: v[0].astype(_F32)[None, :]

    outs = _proj(x, bf(w_in), cos_t, sa_t, sb_t)
    qa, ka2, va2 = outs[:3]
    qkv = [tuple(outs[3 + 3 * g:6 + 3 * g]) for g in range(len(B_GROUPS))]
    o_a = _attn_a(qa, ka2, va2, a_sink[0].astype(_F32),
                  _band_t(3 * BLOCK, BLOCK, 2 * A_WINDOW), _stacked_eye(A_Q_HEADS // A_KV_HEADS),
                  _ones_block_diag(3 * BLOCK))
    o_b = _attn_b(qkv, _band_t(BLOCK + 2 * B_SIDE, B_SIDE, 2 * B_SIDE), _stacked_eye(2),
                  _ones_block_diag(BLOCK + 2 * B_SIDE))

    n = b * s
    h1 = _merge(x.reshape(n, d), o_a.reshape(n, A_Q_W), o_b.reshape(n, B_GW),
                bf(w_gate), row(b_gate), bf(w_br_a), bf(w_br_b), bf(w_out), row(ln1_g), row(ln1_b))
    out = _ffn(h1, bf(w_ff_gate), bf(w_ff_up), bf(w_ff_down), row(ln2_g), row(ln2_b))
    return out.reshape(b, s, d)
```

```python
import functools

import numpy as np
import jax
import jax.numpy as jnp
from jax import lax
from jax.experimental import pallas as pl
from jax.experimental.pallas import tpu as pltpu

D_MODEL = 1024
HEAD_DIM = 64
ROT_DIM = HEAD_DIM // 4
ROPE_THETA = 500000.0
BLOCK = 128
A_Q_HEADS = 16
A_KV_HEADS = 4
A_WINDOW = 128
B_GROUPS = ((128, 1), (512, 4), (2048, 16))
B_HEADS_PER_GROUP = 4
B_SIDE = 64
A_Q_W = A_Q_HEADS * HEAD_DIM
A_KV_W = A_KV_HEADS * HEAD_DIM
B_GW = B_HEADS_PER_GROUP * HEAD_DIM
B_W = len(B_GROUPS) * B_GW
OFF_KA = A_Q_W
OFF_VA = OFF_KA + A_KV_W
OFF_QB = OFF_VA + A_KV_W
OFF_KB = OFF_QB + B_W
OFF_VB = OFF_KB + B_W
DEPTH = 1
DN_ALPHA = (2 * DEPTH) ** 0.25
LN_EPS = 1e-5
NEG = -1e30
LOG2E = 1.4426950408889634
LANES = 128
VMEM_LIMIT = 56 * 1024 * 1024

TM_PROJ = 512
TQ_A = 2048
TS_B = 2048
TM_DENSE = 512
BLOCKS_PER_TRIP_A = 8
LOOKAHEAD_A = 2
LOOKAHEAD_B = 4
UNITS_B = 16
UNITS_FINAL = 8
PHASE_MAJOR_DIL = 16
SLAB_PITCH = BLOCK + 8

_BF16 = jnp.bfloat16
_F32 = jnp.float32
_CONTRACT_LAST = (((1,), (1,)), ((), ()))


def _const_spec(shape):
    nd = len(shape)
    return pl.BlockSpec(shape, lambda *_: (0,) * nd, pipeline_mode=pl.Buffered(1))


def _lane_lt64(rows):
    return lax.broadcasted_iota(jnp.int32, (rows, LANES), 1) < HEAD_DIM


def _edge_variant(tile, blk, n_tiles, n_blks):
    first = jnp.logical_and(tile == 0, blk == 0)
    last = jnp.logical_and(tile == n_tiles - 1, blk == n_blks - 1)
    return first.astype(jnp.int32) + 2 * last.astype(jnp.int32)


def _rope(t, cos, sa, sb):
    return t * cos + pltpu.roll(t, LANES - ROT_DIM // 2, 1) * sa + pltpu.roll(t, ROT_DIM // 2, 1) * sb


def _proj_kernel(x_ref, w_ref, cos_ref, sa_ref, sb_ref,
                 qa_ref, ka_ref, va_ref,
                 q0_ref, k0_ref, v0_ref, q1_ref, k1_ref, v1_ref, q2_ref, k2_ref, v2_ref,
                 slab_ref):
    tm = x_ref.shape[1]
    xb = x_ref[0].astype(_BF16)
    cos = cos_ref[...]
    sa = sa_ref[...]
    sb = sb_ref[...]
    scale = HEAD_DIM ** -0.5 * LOG2E
    lt64 = _lane_lt64(tm)

    def mm(c0, width=2 * LANES):
        return jnp.dot(xb, w_ref[:, c0:c0 + width], preferred_element_type=_F32)

    def mixer_a_q(c):
        r = mm(c * 2 * LANES)
        for h in range(2):
            t = _rope(r[:, h * LANES:(h + 1) * LANES], cos, sa, sb) * scale
            qa_ref[0, :, c * 2 * LANES + h * LANES:c * 2 * LANES + (h + 1) * LANES] = t.astype(_BF16)

    def mixer_a_kv(off, out_ref, rot):
        r = mm(off)
        for h in range(2):
            t = r[:, h * LANES:(h + 1) * LANES]
            if rot:
                t = _rope(t, cos, sa, sb)
            sw = pltpu.roll(t, HEAD_DIM, 1)
            out_ref[0, :, (2 * h) * LANES:(2 * h + 1) * LANES] = jnp.where(lt64, t, sw).astype(_BF16)
            out_ref[0, :, (2 * h + 1) * LANES:(2 * h + 2) * LANES] = jnp.where(lt64, sw, t).astype(_BF16)

    outs = ((q0_ref, k0_ref, v0_ref), (q1_ref, k1_ref, v1_ref), (q2_ref, k2_ref, v2_ref))

    def mixer_b(g, kind):
        dil = B_GROUPS[g][1]
        out_ref = outs[g][kind]
        slab = 2 * (3 * (g - 1) + kind)
        r = mm((OFF_QB, OFF_KB, OFF_VB)[kind] + g * B_GW)
        for h in range(2):
            t = r[:, h * LANES:(h + 1) * LANES]
            if kind < 2:
                t = _rope(t, cos, sa, sb)
            if kind == 0:
                t = t * scale
            if dil == 1:
                out_ref[0, 0, :, h * LANES:(h + 1) * LANES] = t.astype(_BF16)
            elif dil % 16:
                slab_ref[slab + h, 0:tm] = t
            else:
                for k in range(tm // dil):
                    slab_ref[slab + h, k * (dil + 8):k * (dil + 8) + dil] = t[k * dil:(k + 1) * dil]
        if dil > 1:
            rows = tm // dil
            pitch = dil if dil % 16 else dil + 8
            for h in range(2):
                for p in range(dil):
                    v = slab_ref[slab + h, pl.ds(p, rows, stride=pitch), :]
                    out_ref[0, p, :, h * LANES:(h + 1) * LANES] = v.astype(_BF16)

    for c in range(A_Q_W // (2 * LANES)):
        mixer_a_q(c)
    mixer_a_kv(OFF_KA, ka_ref, True)
    mixer_a_kv(OFF_VA, va_ref, False)
    for g in range(len(B_GROUPS)):
        for kind in range(3):
            mixer_b(g, kind)


def _proj(x, w_in_bf, cos_t, sa_t, sb_t):
    b, s, _ = x.shape
    tm = TM_PROJ
    grid = (s // tm, b)
    tok = lambda i, j: (j, i, 0)
    tab = pl.BlockSpec((tm, LANES), lambda i, j: (i, 0))
    out_shape = [jax.ShapeDtypeStruct((b, s, A_Q_W), _BF16),
                 jax.ShapeDtypeStruct((b, s, 2 * A_KV_W), _BF16),
                 jax.ShapeDtypeStruct((b, s, 2 * A_KV_W), _BF16)]
    out_specs = [pl.BlockSpec((1, tm, A_Q_W), tok),
                 pl.BlockSpec((1, tm, 2 * A_KV_W), tok),
                 pl.BlockSpec((1, tm, 2 * A_KV_W), tok)]
    for _, dil in B_GROUPS:
        for _ in range(3):
            out_shape.append(jax.ShapeDtypeStruct((b, dil, s // dil, B_GW), _BF16))
            out_specs.append(pl.BlockSpec((1, dil, tm // dil, B_GW), lambda i, j: (j, 0, i, 0)))
    return pl.pallas_call(
        _proj_kernel,
        out_shape=out_shape,
        grid=grid,
        in_specs=[pl.BlockSpec((1, tm, D_MODEL), tok), _const_spec(w_in_bf.shape), tab, tab, tab],
        out_specs=out_specs,
        scratch_shapes=[pltpu.VMEM((2 * 3 * (len(B_GROUPS) - 1), tm + tm // 2, LANES), _F32)],
        compiler_params=pltpu.CompilerParams(
            dimension_semantics=("arbitrary", "arbitrary"), vmem_limit_bytes=VMEM_LIMIT),
        name="proj",
    )(x, w_in_bf, cos_t, sa_t, sb_t)


def _attn_a_kernel(sink_ref, q_ref, kp_ref, kc_ref, kn_ref, vp_ref, vc_ref, vn_ref, bandt_ref, eye_ref,
                   ones_ref, o_ref, kbuf, vbuf):
    tq = q_ref.shape[1]
    t = pl.program_id(1)
    nblk = tq // BLOCK
    kbuf[0:BLOCK] = kp_ref[0]
    kbuf[BLOCK:BLOCK + tq] = kc_ref[0]
    kbuf[BLOCK + tq:2 * BLOCK + tq] = kn_ref[0]
    vbuf[0:BLOCK] = vp_ref[0]
    vbuf[BLOCK:BLOCK + tq] = vc_ref[0]
    vbuf[BLOCK + tq:2 * BLOCK + tq] = vn_ref[0]
    nkeys = 3 * BLOCK
    grp = A_Q_HEADS // A_KV_HEADS
    lt64_q = _lane_lt64(BLOCK)
    lt64_k = _lane_lt64(nkeys)
    zq = jnp.zeros((BLOCK, LANES), _BF16)
    zk = jnp.zeros((nkeys, LANES), _BF16)
    eye = eye_ref[...]
    ones_bd = ones_ref[...]

    def scores(j, k):
        r0 = pl.multiple_of(j * BLOCK, BLOCK)
        bandt = bandt_ref[_edge_variant(t, j, pl.num_programs(1), nblk)]
        kd = kbuf[pl.ds(r0, nkeys), k * LANES:(k + 1) * LANES]
        qs = []
        for pr in range(2):
            qp = q_ref[0, pl.ds(r0, BLOCK), (2 * k + pr) * LANES:(2 * k + pr + 1) * LANES]
            qs.append(jnp.where(lt64_q, qp, zq))
            qs.append(jnp.where(lt64_q, zq, qp))
        lhs = jnp.concatenate([jnp.concatenate(qs, axis=0), eye], axis=1)
        rhs = jnp.concatenate([kd, bandt], axis=1)
        return lax.dot_general(lhs, rhs, _CONTRACT_LAST, preferred_element_type=_F32)

    def finish(sc, j, k):
        r0 = pl.multiple_of(j * BLOCK, BLOCK)
        ps, sinks = [], []
        for i in range(grp):
            si = sc[i * BLOCK:(i + 1) * BLOCK]
            sk = sink_ref[grp * k + i] * LOG2E
            m = jnp.maximum(jnp.max(si, axis=1, keepdims=True), sk)
            ps.append(jnp.exp2((si - m).astype(_BF16)))
            sinks.append(jnp.exp2(sk - m))
        pst = jnp.concatenate([jnp.concatenate([ps[0], ps[1]], axis=1),
                               jnp.concatenate([ps[2], ps[3]], axis=1)], axis=0)
        vd = vbuf[pl.ds(r0, nkeys), k * LANES:(k + 1) * LANES]
        vbd = jnp.concatenate([jnp.where(lt64_k, vd, zk), jnp.where(lt64_k, zk, vd)], axis=0)
        ol = jnp.dot(pst, jnp.concatenate([vbd, ones_bd], axis=1), preferred_element_type=_F32)
        for pr in range(2):
            rows = slice(pr * BLOCK, (pr + 1) * BLOCK)
            den = ol[rows, LANES:] + jnp.where(lt64_q, sinks[2 * pr], sinks[2 * pr + 1])
            res = ol[rows, :LANES] / den
            o_ref[0, pl.ds(r0, BLOCK), (2 * k + pr) * LANES:(2 * k + pr + 1) * LANES] = res.astype(o_ref.dtype)

    def trip(it, carry):
        work = [(it * BLOCKS_PER_TRIP_A + u, k) for u in range(BLOCKS_PER_TRIP_A) for k in range(A_KV_HEADS)]
        scs = []
        for i in range(len(work) + LOOKAHEAD_A):
            if i < len(work):
                scs.append(scores(*work[i]))
            if i >= LOOKAHEAD_A:
                finish(scs[i - LOOKAHEAD_A], *work[i - LOOKAHEAD_A])
        return carry

    lax.fori_loop(0, nblk // BLOCKS_PER_TRIP_A, trip, 0)


def _attn_a(qa, ka2, va2, sink, bandt, eye, ones_bd):
    b, s, _ = qa.shape
    tq = TQ_A
    nb = tq // BLOCK
    last = s // BLOCK - 1
    cur = lambda i, j: (i, j, 0)
    prev = lambda i, j: (i, jnp.maximum(j * nb - 1, 0), 0)
    nxt = lambda i, j: (i, jnp.minimum((j + 1) * nb, last), 0)
    kvw = ka2.shape[-1]
    kv_specs = [pl.BlockSpec((1, BLOCK, kvw), prev), pl.BlockSpec((1, tq, kvw), cur),
                pl.BlockSpec((1, BLOCK, kvw), nxt)]
    return pl.pallas_call(
        _attn_a_kernel,
        out_shape=jax.ShapeDtypeStruct((b, s, A_Q_W), _BF16),
        grid=(b, s // tq),
        in_specs=[pl.BlockSpec(memory_space=pltpu.SMEM),
                  pl.BlockSpec((1, tq, A_Q_W), cur)] + kv_specs + kv_specs
                 + [_const_spec(bandt.shape), _const_spec(eye.shape), _const_spec(ones_bd.shape)],
        out_specs=pl.BlockSpec((1, tq, A_Q_W), cur),
        scratch_shapes=[pltpu.VMEM((tq + 2 * BLOCK, kvw), _BF16),
                        pltpu.VMEM((tq + 2 * BLOCK, kvw), _BF16)],
        compiler_params=pltpu.CompilerParams(
            dimension_semantics=("arbitrary", "arbitrary"), vmem_limit_bytes=VMEM_LIMIT),
        name="attn_a",
    )(sink, qa, ka2, ka2, ka2, va2, va2, va2, bandt, eye, ones_bd)


_SLAB_O, _SLAB_M, _SLAB_L = 0, 1, 2


def _halo_window(prev_ref, cur_ref, next_ref, p, j, cols):
    nt = cur_ref.shape[2]
    lo, hi = j * BLOCK - B_SIDE, (j + 1) * BLOCK + B_SIDE
    pieces = []
    if lo < 0:
        pieces.append(prev_ref[0, p, :, cols])
    pieces.append(cur_ref[0, p, max(lo, 0):min(hi, nt), cols])
    if hi > nt:
        pieces.append(next_ref[0, p, :, cols])
    return pieces[0] if len(pieces) == 1 else jnp.concatenate(pieces, axis=0)


def _attn_b_kernel(*refs):
    ng = len(B_GROUPS)
    in_refs = refs[:7 * ng]
    bandt_ref, eye_ref, ones_ref, o_ref = refs[7 * ng:7 * ng + 4]
    slab_refs = (None,) + tuple(refs[7 * ng + 4:])
    ts = o_ref.shape[1]
    t = pl.program_id(1)
    nkeys = BLOCK + 2 * B_SIDE
    lt64_q = _lane_lt64(BLOCK)
    lt64_k = _lane_lt64(nkeys)
    zq = jnp.zeros((BLOCK, LANES), _BF16)
    zk = jnp.zeros((nkeys, LANES), _BF16)
    eye = eye_ref[...]
    ones_bd = ones_ref[...]

    def units(g, pjs):
        q_ref, kp_ref, kc_ref, kn_ref, vp_ref, vc_ref, vn_ref = in_refs[7 * g:7 * g + 7]
        nj = ts // B_GROUPS[g][1] // BLOCK
        work = [(p, j, slice(pr * LANES, (pr + 1) * LANES)) for p, j in pjs for pr in range(2)]

        def scores(p, j, cols):
            bandt = bandt_ref[_edge_variant(t, j, pl.num_programs(1), nj)]
            qp = q_ref[0, p, j * BLOCK:(j + 1) * BLOCK, cols]
            kp = _halo_window(kp_ref, kc_ref, kn_ref, p, j, cols)
            qst = jnp.concatenate([jnp.where(lt64_q, qp, zq), jnp.where(lt64_q, zq, qp)], axis=0)
            lhs = jnp.concatenate([qst, eye], axis=1)
            rhs = jnp.concatenate([kp, bandt], axis=1)
            return lax.dot_general(lhs, rhs, _CONTRACT_LAST, preferred_element_type=_F32)

        def finish(sc, p, j, cols):
            m = jnp.max(sc, axis=1, keepdims=True)
            eb = jnp.exp2((sc - m).astype(_BF16))
            pcat = jnp.concatenate([eb[:BLOCK], eb[BLOCK:]], axis=1)
            vp = _halo_window(vp_ref, vc_ref, vn_ref, p, j, cols)
            vbd = jnp.concatenate([jnp.where(lt64_k, vp, zk), jnp.where(lt64_k, zk, vp)], axis=0)
            ol = jnp.dot(pcat, jnp.concatenate([vbd, ones_bd], axis=1), preferred_element_type=_F32)
            return ol[:, :LANES], jnp.where(lt64_q, m[:BLOCK], m[BLOCK:]), ol[:, LANES:]

        scs, res = [], []
        for i in range(len(work) + LOOKAHEAD_B):
            if i < len(work):
                scs.append(scores(*work[i]))
            if i >= LOOKAHEAD_B:
                res.append(finish(scs[i - LOOKAHEAD_B], *work[i - LOOKAHEAD_B]))
        return [res[2 * i:2 * i + 2] for i in range(len(pjs))]

    for g in range(1, ng):
        dil = B_GROUPS[g][1]
        nj = ts // dil // BLOCK
        for u0 in range(0, dil * nj, UNITS_B):
            pjs = [((u0 + i) // nj, (u0 + i) % nj) for i in range(UNITS_B)]
            for (p, j), res in zip(pjs, units(g, pjs)):
                if dil == PHASE_MAJOR_DIL:
                    rows = pl.ds(p * SLAB_PITCH, BLOCK)
                else:
                    rows = pl.ds(p + j * (BLOCK * dil), BLOCK, stride=dil)
                for pr in range(2):
                    for kind in range(3):
                        slab_refs[g][kind * 2 + pr, rows, :] = res[pr][kind]

    def slab_block(g, idx, j):
        dil = B_GROUPS[g][1]
        if dil != PHASE_MAJOR_DIL:
            return slab_refs[g][idx, j * BLOCK:(j + 1) * BLOCK, :]
        per = BLOCK // dil
        pieces = [slab_refs[g][idx, pl.ds((8 * (r % 2)) * SLAB_PITCH + j * per + r // 2, 8, stride=SLAB_PITCH), :]
                  for r in range(BLOCK // 8)]
        return jnp.concatenate(pieces, axis=0)

    for j0 in range(0, ts // BLOCK, UNITS_FINAL):
        pjs = [(0, j0 + i) for i in range(UNITS_FINAL)]
        for (_, j), res in zip(pjs, units(0, pjs)):
            for pr in range(2):
                os_, ms_, ls_ = [res[pr][0]], [res[pr][1]], [res[pr][2]]
                for g in range(1, ng):
                    os_.append(slab_block(g, _SLAB_O * 2 + pr, j))
                    ms_.append(slab_block(g, _SLAB_M * 2 + pr, j))
                    ls_.append(slab_block(g, _SLAB_L * 2 + pr, j))
                mx = functools.reduce(jnp.maximum, ms_)
                num = 0.0
                den = 0.0
                for o, m, l in zip(os_, ms_, ls_):
                    a = jnp.exp2(m - mx)
                    num = num + a * o
                    den = den + a * l
                o_ref[0, j * BLOCK:(j + 1) * BLOCK, pr * LANES:(pr + 1) * LANES] = (num / den).astype(o_ref.dtype)


def _attn_b(qkv, bandt, eye, ones_bd):
    b = qkv[0][0].shape[0]
    s = qkv[0][0].shape[2]
    ts = TS_B
    nt_tiles = s // ts
    args, in_specs = [], []
    for (q, k, v), (_, dil) in zip(qkv, B_GROUPS):
        nt = ts // dil
        nh = nt // B_SIDE
        last = s // dil // B_SIDE - 1
        cur = pl.BlockSpec((1, dil, nt, B_GW), lambda i, j: (i, 0, j, 0))
        prev = pl.BlockSpec((1, dil, B_SIDE, B_GW), lambda i, j, nh=nh: (i, 0, jnp.maximum(j * nh - 1, 0), 0))
        nxt = pl.BlockSpec((1, dil, B_SIDE, B_GW),
                           lambda i, j, nh=nh, last=last: (i, 0, jnp.minimum((j + 1) * nh, last), 0))
        args += [q, k, k, k, v, v, v]
        in_specs += [cur, prev, cur, nxt, prev, cur, nxt]
    slab_shapes = []
    for _, dil in B_GROUPS[1:]:
        rows = dil * SLAB_PITCH if dil == PHASE_MAJOR_DIL else ts
        slab_shapes.append(pltpu.VMEM((3 * 2, rows, LANES), _F32))
    return pl.pallas_call(
        _attn_b_kernel,
        out_shape=jax.ShapeDtypeStruct((b, s, B_GW), _BF16),
        grid=(b, nt_tiles),
        in_specs=in_specs + [_const_spec(bandt.shape), _const_spec(eye.shape), _const_spec(ones_bd.shape)],
        out_specs=pl.BlockSpec((1, ts, B_GW), lambda i, j: (i, j, 0)),
        scratch_shapes=slab_shapes,
        compiler_params=pltpu.CompilerParams(
            dimension_semantics=("arbitrary", "arbitrary"), vmem_limit_bytes=VMEM_LIMIT),
        name="attn_b",
    )(*args, bandt, eye, ones_bd)


def _layer_norm(v, g, b):
    mu = jnp.mean(v, axis=-1, keepdims=True)
    d = v - mu
    var = jnp.mean(d * d, axis=-1, keepdims=True)
    return d * lax.rsqrt(var + LN_EPS) * g + b


def _merge_kernel(x_ref, oa_ref, ob_ref, wg_ref, bg_ref, wa_ref, wb_ref, wo_ref, g_ref, b_ref, h_ref):
    x = x_ref[...]
    xb = x.astype(_BF16)
    gates = jax.nn.sigmoid(jnp.dot(xb, wg_ref[...], preferred_element_type=_F32) + bg_ref[...])
    pa = jnp.dot(oa_ref[...], wa_ref[...], preferred_element_type=_F32)
    pb = jnp.dot(ob_ref[...], wb_ref[...], preferred_element_type=_F32)
    mixed = gates[:, :D_MODEL] * pa + gates[:, D_MODEL:] * pb
    y = jnp.dot(mixed.astype(_BF16), wo_ref[...], preferred_element_type=_F32)
    h_ref[...] = _layer_norm(DN_ALPHA * x + y, g_ref[...], b_ref[...])


def _merge(x2, oa2, ob2, wg, bg, wa, wb, wo, g1, b1):
    n = x2.shape[0]
    tm = TM_DENSE
    row = lambda w: pl.BlockSpec((tm, w), lambda i: (i, 0))
    consts = [wg, bg, wa, wb, wo, g1, b1]
    return pl.pallas_call(
        _merge_kernel,
        out_shape=jax.ShapeDtypeStruct((n, D_MODEL), _F32),
        grid=(n // tm,),
        in_specs=[row(D_MODEL), row(A_Q_W), row(B_GW)] + [_const_spec(c.shape) for c in consts],
        out_specs=row(D_MODEL),
        compiler_params=pltpu.CompilerParams(
            dimension_semantics=("arbitrary",), vmem_limit_bytes=VMEM_LIMIT),
        name="merge",
    )(x2, oa2, ob2, *consts)


def _ffn_kernel(h_ref, wg_ref, wu_ref, wd_ref, g_ref, b_ref, o_ref):
    h = h_ref[...]
    hb = h.astype(_BF16)
    gt = jnp.dot(hb, wg_ref[...], preferred_element_type=_F32)
    up = jnp.dot(hb, wu_ref[...], preferred_element_type=_F32)
    act = (jax.nn.silu(gt) * up).astype(_BF16)
    f = jnp.dot(act, wd_ref[...], preferred_element_type=_F32)
    o_ref[...] = _layer_norm(DN_ALPHA * h + f, g_ref[...], b_ref[...])


def _ffn(h2, wg, wu, wd, g2, b2):
    n = h2.shape[0]
    tm = TM_DENSE
    row = pl.BlockSpec((tm, D_MODEL), lambda i: (i, 0))
    consts = [wg, wu, wd, g2, b2]
    return pl.pallas_call(
        _ffn_kernel,
        out_shape=jax.ShapeDtypeStruct((n, D_MODEL), _F32),
        grid=(n // tm,),
        in_specs=[row] + [_const_spec(c.shape) for c in consts],
        out_specs=row,
        compiler_params=pltpu.CompilerParams(
            dimension_semantics=("arbitrary",), vmem_limit_bytes=VMEM_LIMIT),
        name="ffn",
    )(h2, *consts)


def _rope_tables(s):
    half = ROT_DIM // 2
    inv = np.float32(ROPE_THETA) ** (-np.arange(half, dtype=np.float32) / np.float32(half))
    ang = np.arange(s).astype(np.float32)[:, None] * inv[None, :]
    cos, sin = np.cos(ang).astype(np.float32), np.sin(ang).astype(np.float32)
    zeros = lambda w: np.zeros((s, w), np.float32)
    cos_h = np.concatenate([cos, cos, np.ones((s, HEAD_DIM - ROT_DIM), np.float32)], axis=1)
    sa_h = np.concatenate([-sin, zeros(HEAD_DIM - half)], axis=1)
    sb_h = np.concatenate([zeros(half), sin, zeros(HEAD_DIM - ROT_DIM)], axis=1)
    two = lambda a: jnp.asarray(np.concatenate([a, a], axis=1))
    return two(cos_h), two(sa_h), two(sb_h)


def _band_t(nk, halo, width):
    key = np.arange(nk)[:, None]
    rel = key - np.arange(BLOCK)[None, :]
    band = (rel >= 0) & (rel <= width)
    lead, trail = key >= halo, key < nk - halo
    variants = [band, band & lead, band & trail, band & lead & trail]
    return jnp.asarray(np.where(np.stack(variants), 0.0, NEG).astype(np.float32)).astype(_BF16)


def _ones_block_diag(nkeys):
    lo = np.broadcast_to(np.arange(LANES) < HEAD_DIM, (nkeys, LANES))
    return jnp.asarray(np.concatenate([lo, ~lo], axis=0).astype(np.float32)).astype(_BF16)


def _stacked_eye(heads):
    return jnp.asarray(np.tile(np.eye(BLOCK, dtype=np.float32), (heads, 1))).astype(_BF16)


def kernel(x, w_in, a_sink, w_gate, b_gate, w_br_a, w_br_b, w_out, ln1_g, ln1_b,
           w_ff_gate, w_ff_up, w_ff_down, ln2_g, ln2_b):
    b, s, d = x.shape
    assert d == D_MODEL and s % TS_B == 0 and s % TQ_A == 0 and s % TM_PROJ == 0
    assert w_in.shape[0] == DEPTH == 1
    cos_t, sa_t, sb_t = _rope_tables(s)
    bf = lambda w: w[0].astype(_BF16)
    row = lambda v: v[0].astype(_F32)[None, :]

    outs = _proj(x, bf(w_in), cos_t, sa_t, sb_t)
    qa, ka2, va2 = outs[:3]
    qkv = [tuple(outs[3 + 3 * g:6 + 3 * g]) for g in range(len(B_GROUPS))]
    o_a = _attn_a(qa, ka2, va2, a_sink[0].astype(_F32),
                  _band_t(3 * BLOCK, BLOCK, 2 * A_WINDOW), _stacked_eye(A_Q_HEADS // A_KV_HEADS),
                  _ones_block_diag(3 * BLOCK))
    o_b = _attn_b(qkv, _band_t(BLOCK + 2 * B_SIDE, B_SIDE, 2 * B_SIDE), _stacked_eye(2),
                  _ones_block_diag(BLOCK + 2 * B_SIDE))

    n = b * s
    h1 = _merge(x.reshape(n, d), o_a.reshape(n, A_Q_W), o_b.reshape(n, B_GW),
                bf(w_gate), row(b_gate), bf(w_br_a), bf(w_br_b), bf(w_out), row(ln1_g), row(ln1_b))
    out = _ffn(h1, bf(w_ff_gate), bf(w_ff_up), bf(w_ff_down), row(ln2_g), row(ln2_b))
    return out.reshape(b, s, d)
```

```python
import functools

import numpy as np
import jax
import jax.numpy as jnp
from jax import lax
from jax.experimental import pallas as pl
from jax.experimental.pallas import tpu as pltpu

D_MODEL = 1024
HEAD_DIM = 64
ROT_DIM = HEAD_DIM // 4
ROPE_THETA = 500000.0
BLOCK = 128
A_Q_HEADS = 16
A_KV_HEADS = 4
A_WINDOW = 128
B_GROUPS = ((128, 1), (512, 4), (2048, 16))
B_HEADS_PER_GROUP = 4
B_SIDE = 64
A_Q_W = A_Q_HEADS * HEAD_DIM
A_KV_W = A_KV_HEADS * HEAD_DIM
B_GW = B_HEADS_PER_GROUP * HEAD_DIM
B_W = len(B_GROUPS) * B_GW
OFF_KA = A_Q_W
OFF_VA = OFF_KA + A_KV_W
OFF_QB = OFF_VA + A_KV_W
OFF_KB = OFF_QB + B_W
OFF_VB = OFF_KB + B_W
DEPTH = 1
DN_ALPHA = (2 * DEPTH) ** 0.25
LN_EPS = 1e-5
NEG = -1e30
LOG2E = 1.4426950408889634
LANES = 128
VMEM_LIMIT = 56 * 1024 * 1024

TM_PROJ = 512
TQ_A = 2048
TS_B = 2048
TM_DENSE = 512
LOOKAHEAD_A = 2
LOOKAHEAD_B = 4
UNITS_B = 16
UNITS_FINAL = 8
PHASE_MAJOR_DIL = 16
SLAB_PITCH = BLOCK + 8

_BF16 = jnp.bfloat16
_F32 = jnp.float32
_CONTRACT_LAST = (((1,), (1,)), ((), ()))


def _const_spec(shape):
    nd = len(shape)
    return pl.BlockSpec(shape, lambda *_: (0,) * nd, pipeline_mode=pl.Buffered(1))


def _lane_lt64(rows):
    return lax.broadcasted_iota(jnp.int32, (rows, LANES), 1) < HEAD_DIM


def _edge_variant(tile, blk, n_tiles, n_blks):
    first = jnp.logical_and(tile == 0, blk == 0)
    last = jnp.logical_and(tile == n_tiles - 1, blk == n_blks - 1)
    return first.astype(jnp.int32) + 2 * last.astype(jnp.int32)


def _rope(t, cos, sa, sb):
    return t * cos + pltpu.roll(t, LANES - ROT_DIM // 2, 1) * sa + pltpu.roll(t, ROT_DIM // 2, 1) * sb


def _proj_kernel(x_ref, w_ref, cos_ref, sa_ref, sb_ref,
                 qa_ref, ka_ref, va_ref,
                 q0_ref, k0_ref, v0_ref, q1_ref, k1_ref, v1_ref, q2_ref, k2_ref, v2_ref,
                 slab_ref):
    tm = x_ref.shape[1]
    xb = x_ref[0].astype(_BF16)
    cos = cos_ref[...]
    sa = sa_ref[...]
    sb = sb_ref[...]
    scale = HEAD_DIM ** -0.5 * LOG2E
    lt64 = _lane_lt64(tm)

    def mm(c0, width=2 * LANES):
        return jnp.dot(xb, w_ref[:, c0:c0 + width], preferred_element_type=_F32)

    def mixer_a_q(c):
        r = mm(c * 2 * LANES)
        for h in range(2):
            t = _rope(r[:, h * LANES:(h + 1) * LANES], cos, sa, sb) * scale
            qa_ref[0, :, c * 2 * LANES + h * LANES:c * 2 * LANES + (h + 1) * LANES] = t.astype(_BF16)

    def mixer_a_kv(off, out_ref, rot):
        r = mm(off)
        for h in range(2):
            t = r[:, h * LANES:(h + 1) * LANES]
            if rot:
                t = _rope(t, cos, sa, sb)
            sw = pltpu.roll(t, HEAD_DIM, 1)
            out_ref[0, :, (2 * h) * LANES:(2 * h + 1) * LANES] = jnp.where(lt64, t, sw).astype(_BF16)
            out_ref[0, :, (2 * h + 1) * LANES:(2 * h + 2) * LANES] = jnp.where(lt64, sw, t).astype(_BF16)

    outs = ((q0_ref, k0_ref, v0_ref), (q1_ref, k1_ref, v1_ref), (q2_ref, k2_ref, v2_ref))

    def mixer_b(g, kind):
        dil = B_GROUPS[g][1]
        out_ref = outs[g][kind]
        slab = 2 * (3 * (g - 1) + kind)
        r = mm((OFF_QB, OFF_KB, OFF_VB)[kind] + g * B_GW)
        for h in range(2):
            t = r[:, h * LANES:(h + 1) * LANES]
            if kind < 2:
                t = _rope(t, cos, sa, sb)
            if kind == 0:
                t = t * scale
            if dil == 1:
                out_ref[0, 0, :, h * LANES:(h + 1) * LANES] = t.astype(_BF16)
            elif dil % 16:
                slab_ref[slab + h, 0:tm] = t
            else:
                for k in range(tm // dil):
                    slab_ref[slab + h, k * (dil + 8):k * (dil + 8) + dil] = t[k * dil:(k + 1) * dil]
        if dil > 1:
            rows = tm // dil
            pitch = dil if dil % 16 else dil + 8
            for h in range(2):
                for p in range(dil):
                    v = slab_ref[slab + h, pl.ds(p, rows, stride=pitch), :]
                    out_ref[0, p, :, h * LANES:(h + 1) * LANES] = v.astype(_BF16)

    for c in range(A_Q_W // (2 * LANES)):
        mixer_a_q(c)
    mixer_a_kv(OFF_KA, ka_ref, True)
    mixer_a_kv(OFF_VA, va_ref, False)
    for g in range(len(B_GROUPS)):
        for kind in range(3):
            mixer_b(g, kind)


def _proj(x, w_in_bf, cos_t, sa_t, sb_t):
    b, s, _ = x.shape
    tm = TM_PROJ
    grid = (s // tm, b)
    tok = lambda i, j: (j, i, 0)
    tab = pl.BlockSpec((tm, LANES), lambda i, j: (i, 0))
    out_shape = [jax.ShapeDtypeStruct((b, s, A_Q_W), _BF16),
                 jax.ShapeDtypeStruct((b, s, 2 * A_KV_W), _BF16),
                 jax.ShapeDtypeStruct((b, s, 2 * A_KV_W), _BF16)]
    out_specs = [pl.BlockSpec((1, tm, A_Q_W), tok),
                 pl.BlockSpec((1, tm, 2 * A_KV_W), tok),
                 pl.BlockSpec((1, tm, 2 * A_KV_W), tok)]
    for _, dil in B_GROUPS:
        for _ in range(3):
            out_shape.append(jax.ShapeDtypeStruct((b, dil, s // dil, B_GW), _BF16))
            out_specs.append(pl.BlockSpec((1, dil, tm // dil, B_GW), lambda i, j: (j, 0, i, 0)))
    return pl.pallas_call(
        _proj_kernel,
        out_shape=out_shape,
        grid=grid,
        in_specs=[pl.BlockSpec((1, tm, D_MODEL), tok), _const_spec(w_in_bf.shape), tab, tab, tab],
        out_specs=out_specs,
        scratch_shapes=[pltpu.VMEM((2 * 3 * (len(B_GROUPS) - 1), tm + tm // 2, LANES), _F32)],
        compiler_params=pltpu.CompilerParams(
            dimension_semantics=("arbitrary", "arbitrary"), vmem_limit_bytes=VMEM_LIMIT),
        name="proj",
    )(x, w_in_bf, cos_t, sa_t, sb_t)


def _attn_a_kernel(sink_ref, q_ref, kp_ref, kc_ref, kn_ref, vp_ref, vc_ref, vn_ref, bandt_ref, eye_ref,
                   ones_ref, o_ref):
    tq = q_ref.shape[1]
    t = pl.program_id(1)
    nblk = tq // BLOCK
    nkeys = 3 * BLOCK
    grp = A_Q_HEADS // A_KV_HEADS
    lt64_q = _lane_lt64(BLOCK)
    lt64_k = _lane_lt64(nkeys)
    zq = jnp.zeros((BLOCK, LANES), _BF16)
    zk = jnp.zeros((nkeys, LANES), _BF16)
    eye = eye_ref[...]
    ones_bd = ones_ref[...]

    def window(prev_ref, cur_ref, next_ref, j, k):
        cols = slice(k * LANES, (k + 1) * LANES)
        pieces = []
        if j == 0:
            pieces.append(prev_ref[0, :, cols])
        pieces.append(cur_ref[0, max(j - 1, 0) * BLOCK:min(j + 2, nblk) * BLOCK, cols])
        if j == nblk - 1:
            pieces.append(next_ref[0, :, cols])
        return pieces[0] if len(pieces) == 1 else jnp.concatenate(pieces, axis=0)

    def scores(j, k):
        rows = slice(j * BLOCK, (j + 1) * BLOCK)
        bandt = bandt_ref[_edge_variant(t, j, pl.num_programs(1), nblk)]
        kd = window(kp_ref, kc_ref, kn_ref, j, k)
        qs = []
        for pr in range(2):
            qp = q_ref[0, rows, (2 * k + pr) * LANES:(2 * k + pr + 1) * LANES]
            qs.append(jnp.where(lt64_q, qp, zq))
            qs.append(jnp.where(lt64_q, zq, qp))
        lhs = jnp.concatenate([jnp.concatenate(qs, axis=0), eye], axis=1)
        rhs = jnp.concatenate([kd, bandt], axis=1)
        return lax.dot_general(lhs, rhs, _CONTRACT_LAST, preferred_element_type=_F32)

    def finish(sc, j, k):
        ps, sinks = [], []
        for i in range(grp):
            si = sc[i * BLOCK:(i + 1) * BLOCK]
            sk = sink_ref[grp * k + i] * LOG2E
            m = jnp.maximum(jnp.max(si, axis=1, keepdims=True), sk)
            ps.append(jnp.exp2((si - m).astype(_BF16)))
            sinks.append(jnp.exp2(sk - m))
        pst = jnp.concatenate([jnp.concatenate([ps[0], ps[1]], axis=1),
                               jnp.concatenate([ps[2], ps[3]], axis=1)], axis=0)
        vd = window(vp_ref, vc_ref, vn_ref, j, k)
        vbd = jnp.concatenate([jnp.where(lt64_k, vd, zk), jnp.where(lt64_k, zk, vd)], axis=0)
        ol = jnp.dot(pst, jnp.concatenate([vbd, ones_bd], axis=1), preferred_element_type=_F32)
        for pr in range(2):
            rows = slice(pr * BLOCK, (pr + 1) * BLOCK)
            den = ol[rows, LANES:] + jnp.where(lt64_q, sinks[2 * pr], sinks[2 * pr + 1])
            res = ol[rows, :LANES] / den
            o_ref[0, j * BLOCK:(j + 1) * BLOCK, (2 * k + pr) * LANES:(2 * k + pr + 1) * LANES] = res.astype(o_ref.dtype)

    work = [(j, k) for j in range(nblk) for k in range(A_KV_HEADS)]
    scs = []
    for i in range(len(work) + LOOKAHEAD_A):
        if i < len(work):
            scs.append(scores(*work[i]))
        if i >= LOOKAHEAD_A:
            finish(scs[i - LOOKAHEAD_A], *work[i - LOOKAHEAD_A])


def _attn_a(qa, ka2, va2, sink, bandt, eye, ones_bd):
    b, s, _ = qa.shape
    tq = TQ_A
    nb = tq // BLOCK
    last = s // BLOCK - 1
    cur = lambda i, j: (i, j, 0)
    prev = lambda i, j: (i, jnp.maximum(j * nb - 1, 0), 0)
    nxt = lambda i, j: (i, jnp.minimum((j + 1) * nb, last), 0)
    kvw = ka2.shape[-1]
    kv_specs = [pl.BlockSpec((1, BLOCK, kvw), prev), pl.BlockSpec((1, tq, kvw), cur),
                pl.BlockSpec((1, BLOCK, kvw), nxt)]
    return pl.pallas_call(
        _attn_a_kernel,
        out_shape=jax.ShapeDtypeStruct((b, s, A_Q_W), _BF16),
        grid=(b, s // tq),
        in_specs=[pl.BlockSpec(memory_space=pltpu.SMEM),
                  pl.BlockSpec((1, tq, A_Q_W), cur)] + kv_specs + kv_specs
                 + [_const_spec(bandt.shape), _const_spec(eye.shape), _const_spec(ones_bd.shape)],
        out_specs=pl.BlockSpec((1, tq, A_Q_W), cur),
        compiler_params=pltpu.CompilerParams(
            dimension_semantics=("arbitrary", "arbitrary"), vmem_limit_bytes=VMEM_LIMIT),
        name="attn_a",
    )(sink, qa, ka2, ka2, ka2, va2, va2, va2, bandt, eye, ones_bd)


_SLAB_O, _SLAB_M, _SLAB_L = 0, 1, 2


def _halo_window(prev_ref, cur_ref, next_ref, p, j, cols):
    nt = cur_ref.shape[2]
    lo, hi = j * BLOCK - B_SIDE, (j + 1) * BLOCK + B_SIDE
    pieces = []
    if lo < 0:
        pieces.append(prev_ref[0, p, :, cols])
    pieces.append(cur_ref[0, p, max(lo, 0):min(hi, nt), cols])
    if hi > nt:
        pieces.append(next_ref[0, p, :, cols])
    return pieces[0] if len(pieces) == 1 else jnp.concatenate(pieces, axis=0)


def _attn_b_kernel(*refs):
    ng = len(B_GROUPS)
    in_refs = refs[:7 * ng]
    bandt_ref, eye_ref, ones_ref, o_ref = refs[7 * ng:7 * ng + 4]
    slab_refs = (None,) + tuple(refs[7 * ng + 4:])
    ts = o_ref.shape[1]
    t = pl.program_id(1)
    nkeys = BLOCK + 2 * B_SIDE
    lt64_q = _lane_lt64(BLOCK)
    lt64_k = _lane_lt64(nkeys)
    zq = jnp.zeros((BLOCK, LANES), _BF16)
    zk = jnp.zeros((nkeys, LANES), _BF16)
    eye = eye_ref[...]
    ones_bd = ones_ref[...]

    def units(g, pjs):
        q_ref, kp_ref, kc_ref, kn_ref, vp_ref, vc_ref, vn_ref = in_refs[7 * g:7 * g + 7]
        nj = ts // B_GROUPS[g][1] // BLOCK
        work = [(p, j, slice(pr * LANES, (pr + 1) * LANES)) for p, j in pjs for pr in range(2)]

        def scores(p, j, cols):
            bandt = bandt_ref[_edge_variant(t, j, pl.num_programs(1), nj)]
            qp = q_ref[0, p, j * BLOCK:(j + 1) * BLOCK, cols]
            kp = _halo_window(kp_ref, kc_ref, kn_ref, p, j, cols)
            qst = jnp.concatenate([jnp.where(lt64_q, qp, zq), jnp.where(lt64_q, zq, qp)], axis=0)
            lhs = jnp.concatenate([qst, eye], axis=1)
            rhs = jnp.concatenate([kp, bandt], axis=1)
            return lax.dot_general(lhs, rhs, _CONTRACT_LAST, preferred_element_type=_F32)

        def finish(sc, p, j, cols):
            m = jnp.max(sc, axis=1, keepdims=True)
            eb = jnp.exp2((sc - m).astype(_BF16))
            pcat = jnp.concatenate([eb[:BLOCK], eb[BLOCK:]], axis=1)
            vp = _halo_window(vp_ref, vc_ref, vn_ref, p, j, cols)
            vbd = jnp.concatenate([jnp.where(lt64_k, vp, zk), jnp.where(lt64_k, zk, vp)], axis=0)
            ol = jnp.dot(pcat, jnp.concatenate([vbd, ones_bd], axis=1), preferred_element_type=_F32)
            return ol[:, :LANES], jnp.where(lt64_q, m[:BLOCK], m[BLOCK:]), ol[:, LANES:]

        scs, res = [], []
        for i in range(len(work) + LOOKAHEAD_B):
            if i < len(work):
                scs.append(scores(*work[i]))
            if i >= LOOKAHEAD_B:
                res.append(finish(scs[i - LOOKAHEAD_B], *work[i - LOOKAHEAD_B]))
        return [res[2 * i:2 * i + 2] for i in range(len(pjs))]

    for g in range(1, ng):
        dil = B_GROUPS[g][1]
        nj = ts // dil // BLOCK
        for u0 in range(0, dil * nj, UNITS_B):
            pjs = [((u0 + i) // nj, (u0 + i) % nj) for i in range(UNITS_B)]
            for (p, j), res in zip(pjs, units(g, pjs)):
                if dil == PHASE_MAJOR_DIL:
                    rows = pl.ds(p * SLAB_PITCH, BLOCK)
                else:
                    rows = pl.ds(p + j * (BLOCK * dil), BLOCK, stride=dil)
                for pr in range(2):
                    for kind in range(3):
                        slab_refs[g][kind * 2 + pr, rows, :] = res[pr][kind]

    def slab_block(g, idx, j):
        dil = B_GROUPS[g][1]
        if dil != PHASE_MAJOR_DIL:
            return slab_refs[g][idx, j * BLOCK:(j + 1) * BLOCK, :]
        per = BLOCK // dil
        pieces = [slab_refs[g][idx, pl.ds((8 * (r % 2)) * SLAB_PITCH + j * per + r // 2, 8, stride=SLAB_PITCH), :]
                  for r in range(BLOCK // 8)]
        return jnp.concatenate(pieces, axis=0)

    for j0 in range(0, ts // BLOCK, UNITS_FINAL):
        pjs = [(0, j0 + i) for i in range(UNITS_FINAL)]
        for (_, j), res in zip(pjs, units(0, pjs)):
            for pr in range(2):
                os_, ms_, ls_ = [res[pr][0]], [res[pr][1]], [res[pr][2]]
                for g in range(1, ng):
                    os_.append(slab_block(g, _SLAB_O * 2 + pr, j))
                    ms_.append(slab_block(g, _SLAB_M * 2 + pr, j))
                    ls_.append(slab_block(g, _SLAB_L * 2 + pr, j))
                mx = functools.reduce(jnp.maximum, ms_)
                num = 0.0
                den = 0.0
                for o, m, l in zip(os_, ms_, ls_):
                    a = jnp.exp2(m - mx)
                    num = num + a * o
                    den = den + a * l
                o_ref[0, j * BLOCK:(j + 1) * BLOCK, pr * LANES:(pr + 1) * LANES] = (num / den).astype(o_ref.dtype)


def _attn_b(qkv, bandt, eye, ones_bd):
    b = qkv[0][0].shape[0]
    s = qkv[0][0].shape[2]
    ts = TS_B
    nt_tiles = s // ts
    args, in_specs = [], []
    for (q, k, v), (_, dil) in zip(qkv, B_GROUPS):
        nt = ts // dil
        nh = nt // B_SIDE
        last = s // dil // B_SIDE - 1
        cur = pl.BlockSpec((1, dil, nt, B_GW), lambda i, j: (i, 0, j, 0))
        prev = pl.BlockSpec((1, dil, B_SIDE, B_GW), lambda i, j, nh=nh: (i, 0, jnp.maximum(j * nh - 1, 0), 0))
        nxt = pl.BlockSpec((1, dil, B_SIDE, B_GW),
                           lambda i, j, nh=nh, last=last: (i, 0, jnp.minimum((j + 1) * nh, last), 0))
        args += [q, k, k, k, v, v, v]
        in_specs += [cur, prev, cur, nxt, prev, cur, nxt]
    slab_shapes = []
    for _, dil in B_GROUPS[1:]:
        rows = dil * SLAB_PITCH if dil == PHASE_MAJOR_DIL else ts
        slab_shapes.append(pltpu.VMEM((3 * 2, rows, LANES), _F32))
    return pl.pallas_call(
        _attn_b_kernel,
        out_shape=jax.ShapeDtypeStruct((b, s, B_GW), _BF16),
        grid=(b, nt_tiles),
        in_specs=in_specs + [_const_spec(bandt.shape), _const_spec(eye.shape), _const_spec(ones_bd.shape)],
        out_specs=pl.BlockSpec((1, ts, B_GW), lambda i, j: (i, j, 0)),
        scratch_shapes=slab_shapes,
        compiler_params=pltpu.CompilerParams(
            dimension_semantics=("arbitrary", "arbitrary"), vmem_limit_bytes=VMEM_LIMIT),
        name="attn_b",
    )(*args, bandt, eye, ones_bd)


def _layer_norm(v, g, b):
    mu = jnp.mean(v, axis=-1, keepdims=True)
    d = v - mu
    var = jnp.mean(d * d, axis=-1, keepdims=True)
    return d * lax.rsqrt(var + LN_EPS) * g + b


def _merge_kernel(x_ref, oa_ref, ob_ref, wg_ref, bg_ref, wa_ref, wb_ref, wo_ref, g_ref, b_ref, h_ref):
    x = x_ref[...]
    xb = x.astype(_BF16)
    gates = jax.nn.sigmoid(jnp.dot(xb, wg_ref[...], preferred_element_type=_F32) + bg_ref[...])
    pa = jnp.dot(oa_ref[...], wa_ref[...], preferred_element_type=_F32)
    pb = jnp.dot(ob_ref[...], wb_ref[...], preferred_element_type=_F32)
    mixed = gates[:, :D_MODEL] * pa + gates[:, D_MODEL:] * pb
    y = jnp.dot(mixed.astype(_BF16), wo_ref[...], preferred_element_type=_F32)
    h_ref[...] = _layer_norm(DN_ALPHA * x + y, g_ref[...], b_ref[...])


def _merge(x2, oa2, ob2, wg, bg, wa, wb, wo, g1, b1):
    n = x2.shape[0]
    tm = TM_DENSE
    row = lambda w: pl.BlockSpec((tm, w), lambda i: (i, 0))
    consts = [wg, bg, wa, wb, wo, g1, b1]
    return pl.pallas_call(
        _merge_kernel,
        out_shape=jax.ShapeDtypeStruct((n, D_MODEL), _F32),
        grid=(n // tm,),
        in_specs=[row(D_MODEL), row(A_Q_W), row(B_GW)] + [_const_spec(c.shape) for c in consts],
        out_specs=row(D_MODEL),
        compiler_params=pltpu.CompilerParams(
            dimension_semantics=("arbitrary",), vmem_limit_bytes=VMEM_LIMIT),
        name="merge",
    )(x2, oa2, ob2, *consts)


def _ffn_kernel(h_ref, wg_ref, wu_ref, wd_ref, g_ref, b_ref, o_ref):
    h = h_ref[...]
    hb = h.astype(_BF16)
    gt = jnp.dot(hb, wg_ref[...], preferred_element_type=_F32)
    up = jnp.dot(hb, wu_ref[...], preferred_element_type=_F32)
    act = (jax.nn.silu(gt) * up).astype(_BF16)
    f = jnp.dot(act, wd_ref[...], preferred_element_type=_F32)
    o_ref[...] = _layer_norm(DN_ALPHA * h + f, g_ref[...], b_ref[...])


def _ffn(h2, wg, wu, wd, g2, b2):
    n = h2.shape[0]
    tm = TM_DENSE
    row = pl.BlockSpec((tm, D_MODEL), lambda i: (i, 0))
    consts = [wg, wu, wd, g2, b2]
    return pl.pallas_call(
        _ffn_kernel,
        out_shape=jax.ShapeDtypeStruct((n, D_MODEL), _F32),
        grid=(n // tm,),
        in_specs=[row] + [_const_spec(c.shape) for c in consts],
        out_specs=row,
        compiler_params=pltpu.CompilerParams(
            dimension_semantics=("arbitrary",), vmem_limit_bytes=VMEM_LIMIT),
        name="ffn",
    )(h2, *consts)


def _rope_tables(s):
    half = ROT_DIM // 2
    inv = np.float32(ROPE_THETA) ** (-np.arange(half, dtype=np.float32) / np.float32(half))
    ang = np.arange(s).astype(np.float32)[:, None] * inv[None, :]
    cos, sin = np.cos(ang).astype(np.float32), np.sin(ang).astype(np.float32)
    zeros = lambda w: np.zeros((s, w), np.float32)
    cos_h = np.concatenate([cos, cos, np.ones((s, HEAD_DIM - ROT_DIM), np.float32)], axis=1)
    sa_h = np.concatenate([-sin, zeros(HEAD_DIM - half)], axis=1)
    sb_h = np.concatenate([zeros(half), sin, zeros(HEAD_DIM - ROT_DIM)], axis=1)
    two = lambda a: jnp.asarray(np.concatenate([a, a], axis=1))
    return two(cos_h), two(sa_h), two(sb_h)


def _band_t(nk, halo, width):
    key = np.arange(nk)[:, None]
    rel = key - np.arange(BLOCK)[None, :]
    band = (rel >= 0) & (rel <= width)
    lead, trail = key >= halo, key < nk - halo
    variants = [band, band & lead, band & trail, band & lead & trail]
    return jnp.asarray(np.where(np.stack(variants), 0.0, NEG).astype(np.float32)).astype(_BF16)


def _ones_block_diag(nkeys):
    lo = np.broadcast_to(np.arange(LANES) < HEAD_DIM, (nkeys, LANES))
    return jnp.asarray(np.concatenate([lo, ~lo], axis=0).astype(np.float32)).astype(_BF16)


def _stacked_eye(heads):
    return jnp.asarray(np.tile(np.eye(BLOCK, dtype=np.float32), (heads, 1))).astype(_BF16)


def kernel(x, w_in, a_sink, w_gate, b_gate, w_br_a, w_br_b, w_out, ln1_g, ln1_b,
           w_ff_gate, w_ff_up, w_ff_down, ln2_g, ln2_b):
    b, s, d = x.shape
    assert d == D_MODEL and s % TS_B == 0 and s % TQ_A == 0 and s % TM_PROJ == 0
    assert w_in.shape[0] == DEPTH == 1
    cos_t, sa_t, sb_t = _rope_tables(s)
    bf = lambda w: w[0].astype(_BF16)
    row = lambda v: v[0].astype(_F32)[None, :]

    outs = _proj(x, bf(w_in), cos_t, sa_t, sb_t)
    qa, ka2, va2 = outs[:3]
    qkv = [tuple(outs[3 + 3 * g:6 + 3 * g]) for g in range(len(B_GROUPS))]
    o_a = _attn_a(qa, ka2, va2, a_sink[0].astype(_F32),
                  _band_t(3 * BLOCK, BLOCK, 2 * A_WINDOW), _stacked_eye(A_Q_HEADS // A_KV_HEADS),
                  _ones_block_diag(3 * BLOCK))
    o_b = _attn_b(qkv, _band_t(BLOCK + 2 * B_SIDE, B_SIDE, 2 * B_SIDE), _stacked_eye(2),
                  _ones_block_diag(BLOCK + 2 * B_SIDE))

    n = b * s
    h1 = _merge(x.reshape(n, d), o_a.reshape(n, A_Q_W), o_b.reshape(n, B_GW),
                bf(w_gate), row(b_gate), bf(w_br_a), bf(w_br_b), bf(w_out), row(ln1_g), row(ln1_b))
    out = _ffn(h1, bf(w_ff_gate), bf(w_ff_up), bf(w_ff_down), row(ln2_g), row(ln2_b))
    return out.reshape(b, s, d)
```

```python
import functools

import numpy as np
import jax
import jax.numpy as jnp
from jax import lax
from jax.experimental import pallas as pl
from jax.experimental.pallas import tpu as pltpu

D_MODEL = 1024
HEAD_DIM = 64
ROT_DIM = HEAD_DIM // 4
ROPE_THETA = 500000.0
BLOCK = 128
A_Q_HEADS = 16
A_KV_HEADS = 4
A_WINDOW = 128
B_GROUPS = ((128, 1), (512, 4), (2048, 16))
B_HEADS_PER_GROUP = 4
B_SIDE = 64
A_Q_W = A_Q_HEADS * HEAD_DIM
A_KV_W = A_KV_HEADS * HEAD_DIM
B_GW = B_HEADS_PER_GROUP * HEAD_DIM
B_W = len(B_GROUPS) * B_GW
OFF_KA = A_Q_W
OFF_VA = OFF_KA + A_KV_W
OFF_QB = OFF_VA + A_KV_W
OFF_KB = OFF_QB + B_W
OFF_VB = OFF_KB + B_W
DEPTH = 1
DN_ALPHA = (2 * DEPTH) ** 0.25
LN_EPS = 1e-5
NEG = -1e30
LOG2E = 1.4426950408889634
LANES = 128
VMEM_LIMIT = 56 * 1024 * 1024

TM_PROJ = 512
TQ_A = 2048
TS_B = 2048
TM_DENSE = 512
LOOKAHEAD_A = 2
LOOKAHEAD_B = 4
UNITS_B = 16
UNITS_FINAL = 8
PHASE_MAJOR_DIL = 16
SLAB_PITCH = BLOCK + 8

_BF16 = jnp.bfloat16
_F32 = jnp.float32
_CONTRACT_LAST = (((1,), (1,)), ((), ()))


def _const_spec(shape):
    nd = len(shape)
    return pl.BlockSpec(shape, lambda *_: (0,) * nd, pipeline_mode=pl.Buffered(1))


def _lane_lt64(rows):
    return lax.broadcasted_iota(jnp.int32, (rows, LANES), 1) < HEAD_DIM


def _edge_variant(tile, blk, n_tiles, n_blks):
    first = jnp.logical_and(tile == 0, blk == 0)
    last = jnp.logical_and(tile == n_tiles - 1, blk == n_blks - 1)
    return first.astype(jnp.int32) + 2 * last.astype(jnp.int32)


def _rope(t, cos, sa, sb):
    return t * cos + pltpu.roll(t, LANES - ROT_DIM // 2, 1) * sa + pltpu.roll(t, ROT_DIM // 2, 1) * sb


def _proj_kernel(x_ref, w_ref, cos_ref, sa_ref, sb_ref,
                 qa_ref, ka_ref, va_ref,
                 q0_ref, k0_ref, v0_ref, q1_ref, k1_ref, v1_ref, q2_ref, k2_ref, v2_ref,
                 slab_ref):
    tm = x_ref.shape[1]
    xb = x_ref[0].astype(_BF16)
    cos = cos_ref[...]
    sa = sa_ref[...]
    sb = sb_ref[...]
    scale = HEAD_DIM ** -0.5 * LOG2E
    lt64 = _lane_lt64(tm)

    def mm(c0, width=2 * LANES):
        return jnp.dot(xb, w_ref[:, c0:c0 + width], preferred_element_type=_F32)

    def mixer_a_q(c):
        r = mm(c * 2 * LANES)
        for h in range(2):
            t = _rope(r[:, h * LANES:(h + 1) * LANES], cos, sa, sb) * scale
            qa_ref[0, :, c * 2 * LANES + h * LANES:c * 2 * LANES + (h + 1) * LANES] = t.astype(_BF16)

    def mixer_a_kv(off, out_ref, rot):
        r = mm(off)
        for h in range(2):
            t = r[:, h * LANES:(h + 1) * LANES]
            if rot:
                t = _rope(t, cos, sa, sb)
            sw = pltpu.roll(t, HEAD_DIM, 1)
            out_ref[0, :, (2 * h) * LANES:(2 * h + 1) * LANES] = jnp.where(lt64, t, sw).astype(_BF16)
            out_ref[0, :, (2 * h + 1) * LANES:(2 * h + 2) * LANES] = jnp.where(lt64, sw, t).astype(_BF16)

    outs = ((q0_ref, k0_ref, v0_ref), (q1_ref, k1_ref, v1_ref), (q2_ref, k2_ref, v2_ref))

    def mixer_b(g, kind):
        dil = B_GROUPS[g][1]
        out_ref = outs[g][kind]
        slab = 2 * (3 * (g - 1) + kind)
        r = mm((OFF_QB, OFF_KB, OFF_VB)[kind] + g * B_GW)
        for h in range(2):
            t = r[:, h * LANES:(h + 1) * LANES]
            if kind < 2:
                t = _rope(t, cos, sa, sb)
            if kind == 0:
                t = t * scale
            if dil == 1:
                out_ref[0, 0, :, h * LANES:(h + 1) * LANES] = t.astype(_BF16)
            elif dil % 16:
                slab_ref[slab + h, 0:tm] = t
            else:
                for k in range(tm // dil):
                    slab_ref[slab + h, k * (dil + 8):k * (dil + 8) + dil] = t[k * dil:(k + 1) * dil]
        if dil > 1:
            rows = tm // dil
            pitch = dil if dil % 16 else dil + 8
            for h in range(2):
                for p in range(dil):
                    v = slab_ref[slab + h, pl.ds(p, rows, stride=pitch), :]
                    out_ref[0, p, :, h * LANES:(h + 1) * LANES] = v.astype(_BF16)

    for c in range(A_Q_W // (2 * LANES)):
        mixer_a_q(c)
    mixer_a_kv(OFF_KA, ka_ref, True)
    mixer_a_kv(OFF_VA, va_ref, False)
    for g in range(len(B_GROUPS)):
        for kind in range(3):
            mixer_b(g, kind)


def _proj(x, w_in_bf, cos_t, sa_t, sb_t):
    b, s, _ = x.shape
    tm = TM_PROJ
    grid = (s // tm, b)
    tok = lambda i, j: (j, i, 0)
    tab = pl.BlockSpec((tm, LANES), lambda i, j: (i, 0))
    out_shape = [jax.ShapeDtypeStruct((b, s, A_Q_W), _BF16),
                 jax.ShapeDtypeStruct((b, s, 2 * A_KV_W), _BF16),
                 jax.ShapeDtypeStruct((b, s, 2 * A_KV_W), _BF16)]
    out_specs = [pl.BlockSpec((1, tm, A_Q_W), tok),
                 pl.BlockSpec((1, tm, 2 * A_KV_W), tok),
                 pl.BlockSpec((1, tm, 2 * A_KV_W), tok)]
    for _, dil in B_GROUPS:
        for _ in range(3):
            out_shape.append(jax.ShapeDtypeStruct((b, dil, s // dil, B_GW), _BF16))
            out_specs.append(pl.BlockSpec((1, dil, tm // dil, B_GW), lambda i, j: (j, 0, i, 0)))
    return pl.pallas_call(
        _proj_kernel,
        out_shape=out_shape,
        grid=grid,
        in_specs=[pl.BlockSpec((1, tm, D_MODEL), tok), _const_spec(w_in_bf.shape), tab, tab, tab],
        out_specs=out_specs,
        scratch_shapes=[pltpu.VMEM((2 * 3 * (len(B_GROUPS) - 1), tm + tm // 2, LANES), _F32)],
        compiler_params=pltpu.CompilerParams(
            dimension_semantics=("arbitrary", "arbitrary"), vmem_limit_bytes=VMEM_LIMIT),
        name="proj",
    )(x, w_in_bf, cos_t, sa_t, sb_t)


def _attn_a_kernel(sink_ref, q_ref, kp_ref, kc_ref, kn_ref, vp_ref, vc_ref, vn_ref, bandt_ref, eye_ref,
                   ones_ref, o_ref):
    tq = q_ref.shape[1]
    t = pl.program_id(1)
    nblk = tq // BLOCK
    nkeys = 3 * BLOCK
    grp = A_Q_HEADS // A_KV_HEADS
    lt64_q = _lane_lt64(BLOCK)
    lt64_k = _lane_lt64(nkeys)
    zq = jnp.zeros((BLOCK, LANES), _BF16)
    zk = jnp.zeros((nkeys, LANES), _BF16)
    eye = eye_ref[...]
    ones_bd = ones_ref[...]

    def window(prev_ref, cur_ref, next_ref, j, k):
        cols = slice(k * LANES, (k + 1) * LANES)
        pieces = []
        if j == 0:
            pieces.append(prev_ref[0, :, cols])
        pieces.append(cur_ref[0, max(j - 1, 0) * BLOCK:min(j + 2, nblk) * BLOCK, cols])
        if j == nblk - 1:
            pieces.append(next_ref[0, :, cols])
        return pieces[0] if len(pieces) == 1 else jnp.concatenate(pieces, axis=0)

    def scores(j, k):
        rows = slice(j * BLOCK, (j + 1) * BLOCK)
        bandt = bandt_ref[_edge_variant(t, j, pl.num_programs(1), nblk)]
        kd = window(kp_ref, kc_ref, kn_ref, j, k)
        qs = []
        for pr in range(2):
            qp = q_ref[0, rows, (2 * k + pr) * LANES:(2 * k + pr + 1) * LANES]
            qs.append(jnp.where(lt64_q, qp, zq))
            qs.append(jnp.where(lt64_q, zq, qp))
        lhs = jnp.concatenate([jnp.concatenate(qs, axis=0), eye], axis=1)
        rhs = jnp.concatenate([kd, bandt], axis=1)
        return lax.dot_general(lhs, rhs, _CONTRACT_LAST, preferred_element_type=_F32)

    def finish(sc, j, k):
        ps, sinks = [], []
        for i in range(grp):
            si = sc[i * BLOCK:(i + 1) * BLOCK]
            sk = sink_ref[grp * k + i] * LOG2E
            m = jnp.maximum(jnp.max(si, axis=1, keepdims=True), sk)
            ps.append(jnp.exp2((si - m).astype(_BF16)))
            sinks.append(jnp.exp2(sk - m))
        pst = jnp.concatenate([jnp.concatenate([ps[0], ps[1]], axis=1),
                               jnp.concatenate([ps[2], ps[3]], axis=1)], axis=0)
        vd = window(vp_ref, vc_ref, vn_ref, j, k)
        vbd = jnp.concatenate([jnp.where(lt64_k, vd, zk), jnp.where(lt64_k, zk, vd)], axis=0)
        ol = jnp.dot(pst, jnp.concatenate([vbd, ones_bd], axis=1), preferred_element_type=_F32)
        for pr in range(2):
            rows = slice(pr * BLOCK, (pr + 1) * BLOCK)
            den = ol[rows, LANES:] + jnp.where(lt64_q, sinks[2 * pr], sinks[2 * pr + 1])
            res = ol[rows, :LANES] / den
            o_ref[0, j * BLOCK:(j + 1) * BLOCK, (2 * k + pr) * LANES:(2 * k + pr + 1) * LANES] = res.astype(o_ref.dtype)

    work = [(j, k) for j in range(nblk) for k in range(A_KV_HEADS)]
    scs = []
    for i in range(len(work) + LOOKAHEAD_A):
        if i < len(work):
            scs.append(scores(*work[i]))
        if i >= LOOKAHEAD_A:
            finish(scs[i - LOOKAHEAD_A], *work[i - LOOKAHEAD_A])


def _attn_a(qa, ka2, va2, sink, bandt, eye, ones_bd):
    b, s, _ = qa.shape
    tq = TQ_A
    nb = tq // BLOCK
    last = s // BLOCK - 1
    cur = lambda i, j: (i, j, 0)
    prev = lambda i, j: (i, jnp.maximum(j * nb - 1, 0), 0)
    nxt = lambda i, j: (i, jnp.minimum((j + 1) * nb, last), 0)
    kvw = ka2.shape[-1]
    kv_specs = [pl.BlockSpec((1, BLOCK, kvw), prev), pl.BlockSpec((1, tq, kvw), cur),
                pl.BlockSpec((1, BLOCK, kvw), nxt)]
    return pl.pallas_call(
        _attn_a_kernel,
        out_shape=jax.ShapeDtypeStruct((b, s, A_Q_W), _BF16),
        grid=(b, s // tq),
        in_specs=[pl.BlockSpec(memory_space=pltpu.SMEM),
                  pl.BlockSpec((1, tq, A_Q_W), cur)] + kv_specs + kv_specs
                 + [_const_spec(bandt.shape), _const_spec(eye.shape), _const_spec(ones_bd.shape)],
        out_specs=pl.BlockSpec((1, tq, A_Q_W), cur),
        compiler_params=pltpu.CompilerParams(
            dimension_semantics=("arbitrary", "arbitrary"), vmem_limit_bytes=VMEM_LIMIT),
        name="attn_a",
    )(sink, qa, ka2, ka2, ka2, va2, va2, va2, bandt, eye, ones_bd)


_SLAB_O, _SLAB_M, _SLAB_L = 0, 1, 2


def _halo_window(prev_ref, cur_ref, next_ref, p, j, cols):
    nt = cur_ref.shape[2]
    lo, hi = j * BLOCK - B_SIDE, (j + 1) * BLOCK + B_SIDE
    pieces = []
    if lo < 0:
        pieces.append(prev_ref[0, p, :, cols])
    pieces.append(cur_ref[0, p, max(lo, 0):min(hi, nt), cols])
    if hi > nt:
        pieces.append(next_ref[0, p, :, cols])
    return pieces[0] if len(pieces) == 1 else jnp.concatenate(pieces, axis=0)


def _attn_b_kernel(*refs):
    ng = len(B_GROUPS)
    in_refs = refs[:7 * ng]
    bandt_ref, eye_ref, ones_ref, o_ref = refs[7 * ng:7 * ng + 4]
    slab_refs = (None,) + tuple(refs[7 * ng + 4:])
    ts = o_ref.shape[1]
    t = pl.program_id(1)
    nkeys = BLOCK + 2 * B_SIDE
    lt64_q = _lane_lt64(BLOCK)
    lt64_k = _lane_lt64(nkeys)
    zq = jnp.zeros((BLOCK, LANES), _BF16)
    zk = jnp.zeros((nkeys, LANES), _BF16)
    eye = eye_ref[...]
    ones_bd = ones_ref[...]

    def units(g, pjs):
        q_ref, kp_ref, kc_ref, kn_ref, vp_ref, vc_ref, vn_ref = in_refs[7 * g:7 * g + 7]
        nj = ts // B_GROUPS[g][1] // BLOCK
        work = [(p, j, slice(pr * LANES, (pr + 1) * LANES)) for p, j in pjs for pr in range(2)]

        def scores(p, j, cols):
            bandt = bandt_ref[_edge_variant(t, j, pl.num_programs(1), nj)]
            qp = q_ref[0, p, j * BLOCK:(j + 1) * BLOCK, cols]
            kp = _halo_window(kp_ref, kc_ref, kn_ref, p, j, cols)
            qst = jnp.concatenate([jnp.where(lt64_q, qp, zq), jnp.where(lt64_q, zq, qp)], axis=0)
            lhs = jnp.concatenate([qst, eye], axis=1)
            rhs = jnp.concatenate([kp, bandt], axis=1)
            return lax.dot_general(lhs, rhs, _CONTRACT_LAST, preferred_element_type=_F32)

        def finish(sc, p, j, cols):
            m = jnp.max(sc, axis=1, keepdims=True)
            eb = jnp.exp2((sc - m).astype(_BF16))
            pcat = jnp.concatenate([eb[:BLOCK], eb[BLOCK:]], axis=1)
            vp = _halo_window(vp_ref, vc_ref, vn_ref, p, j, cols)
            vbd = jnp.concatenate([jnp.where(lt64_k, vp, zk), jnp.where(lt64_k, zk, vp)], axis=0)
            ol = jnp.dot(pcat, jnp.concatenate([vbd, ones_bd], axis=1), preferred_element_type=_F32)
            return ol[:, :LANES], jnp.where(lt64_q, m[:BLOCK], m[BLOCK:]), ol[:, LANES:]

        scs, res = [], []
        for i in range(len(work) + LOOKAHEAD_B):
            if i < len(work):
                scs.append(scores(*work[i]))
            if i >= LOOKAHEAD_B:
                res.append(finish(scs[i - LOOKAHEAD_B], *work[i - LOOKAHEAD_B]))
        return [res[2 * i:2 * i + 2] for i in range(len(pjs))]

    for g in range(1, ng):
        dil = B_GROUPS[g][1]
        nj = ts // dil // BLOCK
        for u0 in range(0, dil * nj, UNITS_B):
            pjs = [((u0 + i) // nj, (u0 + i) % nj) for i in range(UNITS_B)]
            for (p, j), res in zip(pjs, units(g, pjs)):
                if dil == PHASE_MAJOR_DIL:
                    rows = pl.ds(p * SLAB_PITCH, BLOCK)
                else:
                    rows = pl.ds(p + j * (BLOCK * dil), BLOCK, stride=dil)
                for pr in range(2):
                    for kind in range(3):
                        slab_refs[g][kind * 2 + pr, rows, :] = res[pr][kind]

    def slab_block(g, idx, j):
        dil = B_GROUPS[g][1]
        if dil != PHASE_MAJOR_DIL:
            return slab_refs[g][idx, j * BLOCK:(j + 1) * BLOCK, :]
        per = BLOCK // dil
        pieces = [slab_refs[g][idx, pl.ds((8 * (r % 2)) * SLAB_PITCH + j * per + r // 2, 8, stride=SLAB_PITCH), :]
                  for r in range(BLOCK // 8)]
        return jnp.concatenate(pieces, axis=0)

    for j0 in range(0, ts // BLOCK, UNITS_FINAL):
        pjs = [(0, j0 + i) for i in range(UNITS_FINAL)]
        for (_, j), res in zip(pjs, units(0, pjs)):
            for pr in range(2):
                os_, ms_, ls_ = [res[pr][0]], [res[pr][1]], [res[pr][2]]
                for g in range(1, ng):
                    os_.append(slab_block(g, _SLAB_O * 2 + pr, j))
                    ms_.append(slab_block(g, _SLAB_M * 2 + pr, j))
                    ls_.append(slab_block(g, _SLAB_L * 2 + pr, j))
                mx = functools.reduce(jnp.maximum, ms_)
                num = 0.0
                den = 0.0
                for o, m, l in zip(os_, ms_, ls_):
                    a = jnp.exp2(m - mx)
                    num = num + a * o
                    den = den + a * l
                o_ref[0, j * BLOCK:(j + 1) * BLOCK, pr * LANES:(pr + 1) * LANES] = (num / den).astype(o_ref.dtype)


def _attn_b(qkv, bandt, eye, ones_bd):
    b = qkv[0][0].shape[0]
    s = qkv[0][0].shape[2]
    ts = TS_B
    nt_tiles = s // ts
    args, in_specs = [], []
    for (q, k, v), (_, dil) in zip(qkv, B_GROUPS):
        nt = ts // dil
        nh = nt // B_SIDE
        last = s // dil // B_SIDE - 1
        cur = pl.BlockSpec((1, dil, nt, B_GW), lambda i, j: (i, 0, j, 0))
        prev = pl.BlockSpec((1, dil, B_SIDE, B_GW), lambda i, j, nh=nh: (i, 0, jnp.maximum(j * nh - 1, 0), 0))
        nxt = pl.BlockSpec((1, dil, B_SIDE, B_GW),
                           lambda i, j, nh=nh, last=last: (i, 0, jnp.minimum((j + 1) * nh, last), 0))
        args += [q, k, k, k, v, v, v]
        in_specs += [cur, prev, cur, nxt, prev, cur, nxt]
    slab_shapes = []
    for _, dil in B_GROUPS[1:]:
        rows = dil * SLAB_PITCH if dil == PHASE_MAJOR_DIL else ts
        slab_shapes.append(pltpu.VMEM((3 * 2, rows, LANES), _F32))
    return pl.pallas_call(
        _attn_b_kernel,
        out_shape=jax.ShapeDtypeStruct((b, s, B_GW), _BF16),
        grid=(b, nt_tiles),
        in_specs=in_specs + [_const_spec(bandt.shape), _const_spec(eye.shape), _const_spec(ones_bd.shape)],
        out_specs=pl.BlockSpec((1, ts, B_GW), lambda i, j: (i, j, 0)),
        scratch_shapes=slab_shapes,
        compiler_params=pltpu.CompilerParams(
            dimension_semantics=("arbitrary", "arbitrary"), vmem_limit_bytes=VMEM_LIMIT),
        name="attn_b",
    )(*args, bandt, eye, ones_bd)


def _layer_norm(v, g, b):
    mu = jnp.mean(v, axis=-1, keepdims=True)
    d = v - mu
    var = jnp.mean(d * d, axis=-1, keepdims=True)
    return d * lax.rsqrt(var + LN_EPS) * g + b


def _after_stores(val, ref, rows):
    probe = jnp.sum(ref[rows, 0:LANES].reshape(-1, 8, LANES), axis=0)
    n = 8 * (4 // val.dtype.itemsize)
    probe = jnp.concatenate([probe] * (n // 8), axis=0).astype(val.dtype)
    head = jnp.where(pl.program_id(0) < 0, probe, val[0:n, 0:LANES])
    top = jnp.concatenate([head, val[0:n, LANES:]], axis=1)
    return jnp.concatenate([top, val[n:]], axis=0)


def _merge_kernel(x_ref, oa_ref, ob_ref, wg_ref, bg_ref, wa_ref, wb_ref, wo_ref, g_ref, b_ref, h_ref):
    half = x_ref.shape[0] // 2
    g = g_ref[...]
    b = b_ref[...]

    def mix(rows):
        xb = x_ref[rows, :].astype(_BF16)
        gates = jax.nn.sigmoid(jnp.dot(xb, wg_ref[...], preferred_element_type=_F32) + bg_ref[...])
        pa = jnp.dot(oa_ref[rows, :], wa_ref[...], preferred_element_type=_F32)
        pb = jnp.dot(ob_ref[rows, :], wb_ref[...], preferred_element_type=_F32)
        return (gates[:, :D_MODEL] * pa + gates[:, D_MODEL:] * pb).astype(_BF16)

    def finish(mixed, rows):
        y = jnp.dot(mixed, wo_ref[...], preferred_element_type=_F32)
        h_ref[rows, :] = _layer_norm(DN_ALPHA * x_ref[rows, :] + y, g, b)

    first, second = slice(0, half), slice(half, 2 * half)
    finish(mix(first), first)
    finish(_after_stores(mix(second), h_ref, first), second)


def _merge(x2, oa2, ob2, wg, bg, wa, wb, wo, g1, b1):
    n = x2.shape[0]
    tm = TM_DENSE
    row = lambda w: pl.BlockSpec((tm, w), lambda i: (i, 0))
    consts = [wg, bg, wa, wb, wo, g1, b1]
    return pl.pallas_call(
        _merge_kernel,
        out_shape=jax.ShapeDtypeStruct((n, D_MODEL), _F32),
        grid=(n // tm,),
        in_specs=[row(D_MODEL), row(A_Q_W), row(B_GW)] + [_const_spec(c.shape) for c in consts],
        out_specs=row(D_MODEL),
        compiler_params=pltpu.CompilerParams(
            dimension_semantics=("arbitrary",), vmem_limit_bytes=VMEM_LIMIT),
        name="merge",
    )(x2, oa2, ob2, *consts)


def _ffn_kernel(h_ref, wg_ref, wu_ref, wd_ref, g_ref, b_ref, o_ref):
    half = h_ref.shape[0] // 2
    g = g_ref[...]
    b = b_ref[...]

    def activations(rows):
        hb = h_ref[rows, :].astype(_BF16)
        gt = jnp.dot(hb, wg_ref[...], preferred_element_type=_F32)
        up = jnp.dot(hb, wu_ref[...], preferred_element_type=_F32)
        return (jax.nn.silu(gt) * up).astype(_BF16)

    def finish(act, rows):
        f = jnp.dot(act, wd_ref[...], preferred_element_type=_F32)
        o_ref[rows, :] = _layer_norm(DN_ALPHA * h_ref[rows, :] + f, g, b)

    first, second = slice(0, half), slice(half, 2 * half)
    finish(activations(first), first)
    finish(_after_stores(activations(second), o_ref, first), second)


def _ffn(h2, wg, wu, wd, g2, b2):
    n = h2.shape[0]
    tm = TM_DENSE
    row = pl.BlockSpec((tm, D_MODEL), lambda i: (i, 0))
    consts = [wg, wu, wd, g2, b2]
    return pl.pallas_call(
        _ffn_kernel,
        out_shape=jax.ShapeDtypeStruct((n, D_MODEL), _F32),
        grid=(n // tm,),
        in_specs=[row] + [_const_spec(c.shape) for c in consts],
        out_specs=row,
        compiler_params=pltpu.CompilerParams(
            dimension_semantics=("arbitrary",), vmem_limit_bytes=VMEM_LIMIT),
        name="ffn",
    )(h2, *consts)


def _rope_tables(s):
    half = ROT_DIM // 2
    inv = np.float32(ROPE_THETA) ** (-np.arange(half, dtype=np.float32) / np.float32(half))
    ang = np.arange(s).astype(np.float32)[:, None] * inv[None, :]
    cos, sin = np.cos(ang).astype(np.float32), np.sin(ang).astype(np.float32)
    zeros = lambda w: np.zeros((s, w), np.float32)
    cos_h = np.concatenate([cos, cos, np.ones((s, HEAD_DIM - ROT_DIM), np.float32)], axis=1)
    sa_h = np.concatenate([-sin, zeros(HEAD_DIM - half)], axis=1)
    sb_h = np.concatenate([zeros(half), sin, zeros(HEAD_DIM - ROT_DIM)], axis=1)
    two = lambda a: jnp.asarray(np.concatenate([a, a], axis=1))
    return two(cos_h), two(sa_h), two(sb_h)


def _band_t(nk, halo, width):
    key = np.arange(nk)[:, None]
    rel = key - np.arange(BLOCK)[None, :]
    band = (rel >= 0) & (rel <= width)
    lead, trail = key >= halo, key < nk - halo
    variants = [band, band & lead, band & trail, band & lead & trail]
    return jnp.asarray(np.where(np.stack(variants), 0.0, NEG).astype(np.float32)).astype(_BF16)


def _ones_block_diag(nkeys):
    lo = np.broadcast_to(np.arange(LANES) < HEAD_DIM, (nkeys, LANES))
    return jnp.asarray(np.concatenate([lo, ~lo], axis=0).astype(np.float32)).astype(_BF16)


def _stacked_eye(heads):
    return jnp.asarray(np.tile(np.eye(BLOCK, dtype=np.float32), (heads, 1))).astype(_BF16)


def kernel(x, w_in, a_sink, w_gate, b_gate, w_br_a, w_br_b, w_out, ln1_g, ln1_b,
           w_ff_gate, w_ff_up, w_ff_down, ln2_g, ln2_b):
    b, s, d = x.shape
    assert d == D_MODEL and s % TS_B == 0 and s % TQ_A == 0 and s % TM_PROJ == 0
    assert w_in.shape[0] == DEPTH == 1
    cos_t, sa_t, sb_t = _rope_tables(s)
    bf = lambda w: w[0].astype(_BF16)
    row = lambda v: v[0].astype(_F32)[None, :]

    outs = _proj(x, bf(w_in), cos_t, sa_t, sb_t)
    qa, ka2, va2 = outs[:3]
    qkv = [tuple(outs[3 + 3 * g:6 + 3 * g]) for g in range(len(B_GROUPS))]
    o_a = _attn_a(qa, ka2, va2, a_sink[0].astype(_F32),
                  _band_t(3 * BLOCK, BLOCK, 2 * A_WINDOW), _stacked_eye(A_Q_HEADS // A_KV_HEADS),
                  _ones_block_diag(3 * BLOCK))
    o_b = _attn_b(qkv, _band_t(BLOCK + 2 * B_SIDE, B_SIDE, 2 * B_SIDE), _stacked_eye(2),
                  _ones_block_diag(BLOCK + 2 * B_SIDE))

    n = b * s
    h1 = _merge(x.reshape(n, d), o_a.reshape(n, A_Q_W), o_b.reshape(n, B_GW),
                bf(w_gate), row(b_gate), bf(w_br_a), bf(w_br_b), bf(w_out), row(ln1_g), row(ln1_b))
    out = _ffn(h1, bf(w_ff_gate), bf(w_ff_up), bf(w_ff_down), row(ln2_g), row(ln2_b))
    return out.reshape(b, s, d)
```

```python
import functools

import numpy as np
import jax
import jax.numpy as jnp
from jax import lax
from jax.experimental import pallas as pl
from jax.experimental.pallas import tpu as pltpu

D_MODEL = 1024
HEAD_DIM = 64
ROT_DIM = HEAD_DIM // 4
ROPE_THETA = 500000.0
BLOCK = 128
A_Q_HEADS = 16
A_KV_HEADS = 4
A_WINDOW = 128
B_GROUPS = ((128, 1), (512, 4), (2048, 16))
B_HEADS_PER_GROUP = 4
B_SIDE = 64
A_Q_W = A_Q_HEADS * HEAD_DIM
A_KV_W = A_KV_HEADS * HEAD_DIM
B_GW = B_HEADS_PER_GROUP * HEAD_DIM
B_W = len(B_GROUPS) * B_GW
OFF_KA = A_Q_W
OFF_VA = OFF_KA + A_KV_W
OFF_QB = OFF_VA + A_KV_W
OFF_KB = OFF_QB + B_W
OFF_VB = OFF_KB + B_W
DEPTH = 1
DN_ALPHA = (2 * DEPTH) ** 0.25
LN_EPS = 1e-5
NEG = -1e30
LOG2E = 1.4426950408889634
LANES = 128
VMEM_LIMIT = 56 * 1024 * 1024

TM_PROJ = 512
TQ_A = 2048
TS_B = 2048
TM_DENSE = 512
LOOKAHEAD_A = 2
LOOKAHEAD_B = 4
UNITS_B = 16
UNITS_FINAL = 8
PHASE_MAJOR_DIL = 16
SLAB_PITCH = BLOCK + 8

_BF16 = jnp.bfloat16
_F32 = jnp.float32
_CONTRACT_LAST = (((1,), (1,)), ((), ()))


def _const_spec(shape):
    nd = len(shape)
    return pl.BlockSpec(shape, lambda *_: (0,) * nd, pipeline_mode=pl.Buffered(1))


def _lane_lt64(rows):
    return lax.broadcasted_iota(jnp.int32, (rows, LANES), 1) < HEAD_DIM


def _edge_variant(tile, blk, n_tiles, n_blks):
    first = jnp.logical_and(tile == 0, blk == 0)
    last = jnp.logical_and(tile == n_tiles - 1, blk == n_blks - 1)
    return first.astype(jnp.int32) + 2 * last.astype(jnp.int32)


def _rope(t, cos, sa, sb):
    return t * cos + pltpu.roll(t, LANES - ROT_DIM // 2, 1) * sa + pltpu.roll(t, ROT_DIM // 2, 1) * sb


def _proj_kernel(x_ref, w_ref, cos_ref, sa_ref, sb_ref,
                 qa_ref, ka_ref, va_ref,
                 q0_ref, k0_ref, v0_ref, q1_ref, k1_ref, v1_ref, q2_ref, k2_ref, v2_ref,
                 slab_ref):
    tm = x_ref.shape[1]
    xb = x_ref[0].astype(_BF16)
    cos = cos_ref[...]
    sa = sa_ref[...]
    sb = sb_ref[...]
    scale = HEAD_DIM ** -0.5 * LOG2E
    lt64 = _lane_lt64(tm)

    def mm(c0, width=2 * LANES):
        return jnp.dot(xb, w_ref[:, c0:c0 + width], preferred_element_type=_F32)

    def mixer_a_q(c):
        r = mm(c * 2 * LANES)
        for h in range(2):
            t = _rope(r[:, h * LANES:(h + 1) * LANES], cos, sa, sb) * scale
            qa_ref[0, :, c * 2 * LANES + h * LANES:c * 2 * LANES + (h + 1) * LANES] = t.astype(_BF16)

    def mixer_a_kv(off, out_ref, rot):
        r = mm(off)
        for h in range(2):
            t = r[:, h * LANES:(h + 1) * LANES]
            if rot:
                t = _rope(t, cos, sa, sb)
            sw = pltpu.roll(t, HEAD_DIM, 1)
            out_ref[0, :, (2 * h) * LANES:(2 * h + 1) * LANES] = jnp.where(lt64, t, sw).astype(_BF16)
            out_ref[0, :, (2 * h + 1) * LANES:(2 * h + 2) * LANES] = jnp.where(lt64, sw, t).astype(_BF16)

    outs = ((q0_ref, k0_ref, v0_ref), (q1_ref, k1_ref, v1_ref), (q2_ref, k2_ref, v2_ref))

    def mixer_b(g, kind):
        dil = B_GROUPS[g][1]
        out_ref = outs[g][kind]
        slab = 2 * (3 * (g - 1) + kind)
        r = mm((OFF_QB, OFF_KB, OFF_VB)[kind] + g * B_GW)
        for h in range(2):
            t = r[:, h * LANES:(h + 1) * LANES]
            if kind < 2:
                t = _rope(t, cos, sa, sb)
            if kind == 0:
                t = t * scale
            if dil == 1:
                out_ref[0, 0, :, h * LANES:(h + 1) * LANES] = t.astype(_BF16)
            elif dil % 16:
                slab_ref[slab + h, 0:tm] = t
            else:
                for k in range(tm // dil):
                    slab_ref[slab + h, k * (dil + 8):k * (dil + 8) + dil] = t[k * dil:(k + 1) * dil]
        if dil > 1:
            rows = tm // dil
            pitch = dil if dil % 16 else dil + 8
            for h in range(2):
                for p in range(dil):
                    v = slab_ref[slab + h, pl.ds(p, rows, stride=pitch), :]
                    out_ref[0, p, :, h * LANES:(h + 1) * LANES] = v.astype(_BF16)

    for c in range(A_Q_W // (2 * LANES)):
        mixer_a_q(c)
    mixer_a_kv(OFF_KA, ka_ref, True)
    mixer_a_kv(OFF_VA, va_ref, False)
    for g in range(len(B_GROUPS)):
        for kind in range(3):
            mixer_b(g, kind)


def _proj(x, w_in_bf, cos_t, sa_t, sb_t):
    b, s, _ = x.shape
    tm = TM_PROJ
    grid = (s // tm, b)
    tok = lambda i, j: (j, i, 0)
    tab = pl.BlockSpec((tm, LANES), lambda i, j: (i, 0))
    out_shape = [jax.ShapeDtypeStruct((b, s, A_Q_W), _BF16),
                 jax.ShapeDtypeStruct((b, s, 2 * A_KV_W), _BF16),
                 jax.ShapeDtypeStruct((b, s, 2 * A_KV_W), _BF16)]
    out_specs = [pl.BlockSpec((1, tm, A_Q_W), tok),
                 pl.BlockSpec((1, tm, 2 * A_KV_W), tok),
                 pl.BlockSpec((1, tm, 2 * A_KV_W), tok)]
    for _, dil in B_GROUPS:
        for _ in range(3):
            out_shape.append(jax.ShapeDtypeStruct((b, dil, s // dil, B_GW), _BF16))
            out_specs.append(pl.BlockSpec((1, dil, tm // dil, B_GW), lambda i, j: (j, 0, i, 0)))
    return pl.pallas_call(
        _proj_kernel,
        out_shape=out_shape,
        grid=grid,
        in_specs=[pl.BlockSpec((1, tm, D_MODEL), tok), _const_spec(w_in_bf.shape), tab, tab, tab],
        out_specs=out_specs,
        scratch_shapes=[pltpu.VMEM((2 * 3 * (len(B_GROUPS) - 1), tm + tm // 2, LANES), _F32)],
        compiler_params=pltpu.CompilerParams(
            dimension_semantics=("arbitrary", "arbitrary"), vmem_limit_bytes=VMEM_LIMIT),
        name="proj",
    )(x, w_in_bf, cos_t, sa_t, sb_t)


def _attn_a_kernel(sink_ref, q_ref, kp_ref, kc_ref, kn_ref, vp_ref, vc_ref, vn_ref, bandt_ref, eye_ref,
                   ones_ref, o_ref):
    tq = q_ref.shape[1]
    t = pl.program_id(1)
    nblk = tq // BLOCK
    nkeys = 3 * BLOCK
    grp = A_Q_HEADS // A_KV_HEADS
    lt64_q = _lane_lt64(BLOCK)
    lt64_k = _lane_lt64(nkeys)
    zq = jnp.zeros((BLOCK, LANES), _BF16)
    zk = jnp.zeros((nkeys, LANES), _BF16)
    eye = eye_ref[...]
    ones_bd = ones_ref[...]

    def window(prev_ref, cur_ref, next_ref, j, k):
        cols = slice(k * LANES, (k + 1) * LANES)
        pieces = []
        if j == 0:
            pieces.append(prev_ref[0, :, cols])
        pieces.append(cur_ref[0, max(j - 1, 0) * BLOCK:min(j + 2, nblk) * BLOCK, cols])
        if j == nblk - 1:
            pieces.append(next_ref[0, :, cols])
        return pieces[0] if len(pieces) == 1 else jnp.concatenate(pieces, axis=0)

    def scores(j, k):
        rows = slice(j * BLOCK, (j + 1) * BLOCK)
        bandt = bandt_ref[_edge_variant(t, j, pl.num_programs(1), nblk)]
        kd = window(kp_ref, kc_ref, kn_ref, j, k)
        qs = []
        for pr in range(2):
            qp = q_ref[0, rows, (2 * k + pr) * LANES:(2 * k + pr + 1) * LANES]
            qs.append(jnp.where(lt64_q, qp, zq))
            qs.append(jnp.where(lt64_q, zq, qp))
        lhs = jnp.concatenate([jnp.concatenate(qs, axis=0), eye], axis=1)
        rhs = jnp.concatenate([kd, bandt], axis=1)
        return lax.dot_general(lhs, rhs, _CONTRACT_LAST, preferred_element_type=_F32)

    def finish(sc, j, k):
        ps, sinks = [], []
        for i in range(grp):
            si = sc[i * BLOCK:(i + 1) * BLOCK]
            sk = sink_ref[grp * k + i] * LOG2E
            m = jnp.maximum(jnp.max(si, axis=1, keepdims=True), sk)
            ps.append(jnp.exp2((si - m).astype(_BF16)))
            sinks.append(jnp.exp2(sk - m))
        pst = jnp.concatenate([jnp.concatenate([ps[0], ps[1]], axis=1),
                               jnp.concatenate([ps[2], ps[3]], axis=1)], axis=0)
        vd = window(vp_ref, vc_ref, vn_ref, j, k)
        vbd = jnp.concatenate([jnp.where(lt64_k, vd, zk), jnp.where(lt64_k, zk, vd)], axis=0)
        ol = jnp.dot(pst, jnp.concatenate([vbd, ones_bd], axis=1), preferred_element_type=_F32)
        for pr in range(2):
            rows = slice(pr * BLOCK, (pr + 1) * BLOCK)
            den = ol[rows, LANES:] + jnp.where(lt64_q, sinks[2 * pr], sinks[2 * pr + 1])
            res = ol[rows, :LANES] / den
            o_ref[0, j * BLOCK:(j + 1) * BLOCK, (2 * k + pr) * LANES:(2 * k + pr + 1) * LANES] = res.astype(o_ref.dtype)

    work = [(j, k) for j in range(nblk) for k in range(A_KV_HEADS)]
    scs = []
    for i in range(len(work) + LOOKAHEAD_A):
        if i < len(work):
            scs.append(scores(*work[i]))
        if i >= LOOKAHEAD_A:
            finish(scs[i - LOOKAHEAD_A], *work[i - LOOKAHEAD_A])


def _attn_a(qa, ka2, va2, sink, bandt, eye, ones_bd):
    b, s, _ = qa.shape
    tq = TQ_A
    nb = tq // BLOCK
    last = s // BLOCK - 1
    cur = lambda i, j: (i, j, 0)
    prev = lambda i, j: (i, jnp.maximum(j * nb - 1, 0), 0)
    nxt = lambda i, j: (i, jnp.minimum((j + 1) * nb, last), 0)
    kvw = ka2.shape[-1]
    kv_specs = [pl.BlockSpec((1, BLOCK, kvw), prev), pl.BlockSpec((1, tq, kvw), cur),
                pl.BlockSpec((1, BLOCK, kvw), nxt)]
    return pl.pallas_call(
        _attn_a_kernel,
        out_shape=jax.ShapeDtypeStruct((b, s, A_Q_W), _BF16),
        grid=(b, s // tq),
        in_specs=[pl.BlockSpec(memory_space=pltpu.SMEM),
                  pl.BlockSpec((1, tq, A_Q_W), cur)] + kv_specs + kv_specs
                 + [_const_spec(bandt.shape), _const_spec(eye.shape), _const_spec(ones_bd.shape)],
        out_specs=pl.BlockSpec((1, tq, A_Q_W), cur),
        compiler_params=pltpu.CompilerParams(
            dimension_semantics=("arbitrary", "arbitrary"), vmem_limit_bytes=VMEM_LIMIT),
        name="attn_a",
    )(sink, qa, ka2, ka2, ka2, va2, va2, va2, bandt, eye, ones_bd)


_SLAB_O, _SLAB_M, _SLAB_L = 0, 1, 2


def _halo_window(prev_ref, cur_ref, next_ref, p, j, cols):
    nt = cur_ref.shape[2]
    lo, hi = j * BLOCK - B_SIDE, (j + 1) * BLOCK + B_SIDE
    pieces = []
    if lo < 0:
        pieces.append(prev_ref[0, p, :, cols])
    pieces.append(cur_ref[0, p, max(lo, 0):min(hi, nt), cols])
    if hi > nt:
        pieces.append(next_ref[0, p, :, cols])
    return pieces[0] if len(pieces) == 1 else jnp.concatenate(pieces, axis=0)


def _attn_b_kernel(*refs):
    ng = len(B_GROUPS)
    in_refs = refs[:7 * ng]
    bandt_ref, eye_ref, ones_ref, o_ref = refs[7 * ng:7 * ng + 4]
    slab_refs = (None,) + tuple(refs[7 * ng + 4:])
    ts = o_ref.shape[1]
    t = pl.program_id(1)
    nkeys = BLOCK + 2 * B_SIDE
    lt64_q = _lane_lt64(BLOCK)
    lt64_k = _lane_lt64(nkeys)
    zq = jnp.zeros((BLOCK, LANES), _BF16)
    zk = jnp.zeros((nkeys, LANES), _BF16)
    eye = eye_ref[...]
    ones_bd = ones_ref[...]

    def units(g, pjs):
        q_ref, kp_ref, kc_ref, kn_ref, vp_ref, vc_ref, vn_ref = in_refs[7 * g:7 * g + 7]
        nj = ts // B_GROUPS[g][1] // BLOCK
        work = [(p, j, slice(pr * LANES, (pr + 1) * LANES)) for p, j in pjs for pr in range(2)]

        def scores(p, j, cols):
            bandt = bandt_ref[_edge_variant(t, j, pl.num_programs(1), nj)]
            qp = q_ref[0, p, j * BLOCK:(j + 1) * BLOCK, cols]
            kp = _halo_window(kp_ref, kc_ref, kn_ref, p, j, cols)
            qst = jnp.concatenate([jnp.where(lt64_q, qp, zq), jnp.where(lt64_q, zq, qp)], axis=0)
            lhs = jnp.concatenate([qst, eye], axis=1)
            rhs = jnp.concatenate([kp, bandt], axis=1)
            return lax.dot_general(lhs, rhs, _CONTRACT_LAST, preferred_element_type=_F32)

        def finish(sc, p, j, cols):
            m = jnp.max(sc, axis=1, keepdims=True)
            eb = jnp.exp2((sc - m).astype(_BF16))
            pcat = jnp.concatenate([eb[:BLOCK], eb[BLOCK:]], axis=1)
            vp = _halo_window(vp_ref, vc_ref, vn_ref, p, j, cols)
            vbd = jnp.concatenate([jnp.where(lt64_k, vp, zk), jnp.where(lt64_k, zk, vp)], axis=0)
            ol = jnp.dot(pcat, jnp.concatenate([vbd, ones_bd], axis=1), preferred_element_type=_F32)
            return ol[:, :LANES], jnp.where(lt64_q, m[:BLOCK], m[BLOCK:]), ol[:, LANES:]

        scs, res = [], []
        for i in range(len(work) + LOOKAHEAD_B):
            if i < len(work):
                scs.append(scores(*work[i]))
            if i >= LOOKAHEAD_B:
                res.append(finish(scs[i - LOOKAHEAD_B], *work[i - LOOKAHEAD_B]))
        return [res[2 * i:2 * i + 2] for i in range(len(pjs))]

    for g in range(1, ng):
        dil = B_GROUPS[g][1]
        nj = ts // dil // BLOCK
        for u0 in range(0, dil * nj, UNITS_B):
            pjs = [((u0 + i) // nj, (u0 + i) % nj) for i in range(UNITS_B)]
            for (p, j), res in zip(pjs, units(g, pjs)):
                if dil == PHASE_MAJOR_DIL:
                    rows = pl.ds(p * SLAB_PITCH, BLOCK)
                else:
                    rows = pl.ds(p + j * (BLOCK * dil), BLOCK, stride=dil)
                for pr in range(2):
                    for kind in range(3):
                        slab_refs[g][kind * 2 + pr, rows, :] = res[pr][kind]

    def slab_block(g, idx, j):
        dil = B_GROUPS[g][1]
        if dil != PHASE_MAJOR_DIL:
            return slab_refs[g][idx, j * BLOCK:(j + 1) * BLOCK, :]
        per = BLOCK // dil
        pieces = [slab_refs[g][idx, pl.ds((8 * (r % 2)) * SLAB_PITCH + j * per + r // 2, 8, stride=SLAB_PITCH), :]
                  for r in range(BLOCK // 8)]
        return jnp.concatenate(pieces, axis=0)

    for j0 in range(0, ts // BLOCK, UNITS_FINAL):
        pjs = [(0, j0 + i) for i in range(UNITS_FINAL)]
        for (_, j), res in zip(pjs, units(0, pjs)):
            for pr in range(2):
                os_, ms_, ls_ = [res[pr][0]], [res[pr][1]], [res[pr][2]]
                for g in range(1, ng):
                    os_.append(slab_block(g, _SLAB_O * 2 + pr, j))
                    ms_.append(slab_block(g, _SLAB_M * 2 + pr, j))
                    ls_.append(slab_block(g, _SLAB_L * 2 + pr, j))
                mx = functools.reduce(jnp.maximum, ms_)
                num = 0.0
                den = 0.0
                for o, m, l in zip(os_, ms_, ls_):
                    a = jnp.exp2(m - mx)
                    num = num + a * o
                    den = den + a * l
                o_ref[0, j * BLOCK:(j + 1) * BLOCK, pr * LANES:(pr + 1) * LANES] = (num / den).astype(o_ref.dtype)


def _attn_b(qkv, bandt, eye, ones_bd):
    b = qkv[0][0].shape[0]
    s = qkv[0][0].shape[2]
    ts = TS_B
    nt_tiles = s // ts
    args, in_specs = [], []
    for (q, k, v), (_, dil) in zip(qkv, B_GROUPS):
        nt = ts // dil
        nh = nt // B_SIDE
        last = s // dil // B_SIDE - 1
        cur = pl.BlockSpec((1, dil, nt, B_GW), lambda i, j: (i, 0, j, 0))
        prev = pl.BlockSpec((1, dil, B_SIDE, B_GW), lambda i, j, nh=nh: (i, 0, jnp.maximum(j * nh - 1, 0), 0))
        nxt = pl.BlockSpec((1, dil, B_SIDE, B_GW),
                           lambda i, j, nh=nh, last=last: (i, 0, jnp.minimum((j + 1) * nh, last), 0))
        args += [q, k, k, k, v, v, v]
        in_specs += [cur, prev, cur, nxt, prev, cur, nxt]
    slab_shapes = []
    for _, dil in B_GROUPS[1:]:
        rows = dil * SLAB_PITCH if dil == PHASE_MAJOR_DIL else ts
        slab_shapes.append(pltpu.VMEM((3 * 2, rows, LANES), _F32))
    return pl.pallas_call(
        _attn_b_kernel,
        out_shape=jax.ShapeDtypeStruct((b, s, B_GW), _BF16),
        grid=(b, nt_tiles),
        in_specs=in_specs + [_const_spec(bandt.shape), _const_spec(eye.shape), _const_spec(ones_bd.shape)],
        out_specs=pl.BlockSpec((1, ts, B_GW), lambda i, j: (i, j, 0)),
        scratch_shapes=slab_shapes,
        compiler_params=pltpu.CompilerParams(
            dimension_semantics=("arbitrary", "arbitrary"), vmem_limit_bytes=VMEM_LIMIT),
        name="attn_b",
    )(*args, bandt, eye, ones_bd)


def _layer_norm(v, g, b):
    mu = jnp.mean(v, axis=-1, keepdims=True)
    d = v - mu
    var = jnp.mean(d * d, axis=-1, keepdims=True)
    return d * lax.rsqrt(var + LN_EPS) * g + b


def _after_stores(val, ref, rows):
    probe = jnp.sum(ref[rows, 0:LANES].reshape(-1, 8, LANES), axis=0)
    n = 8 * (4 // val.dtype.itemsize)
    probe = jnp.concatenate([probe] * (n // 8), axis=0).astype(val.dtype)
    head = jnp.where(pl.program_id(0) < 0, probe, val[0:n, 0:LANES])
    top = jnp.concatenate([head, val[0:n, LANES:]], axis=1)
    return jnp.concatenate([top, val[n:]], axis=0)


def _tail_kernel(x_ref, oa_ref, ob_ref, wg_ref, bg_ref, wa_ref, wb_ref, wo_ref, g1_ref, b1_ref,
                 wfg_ref, wfu_ref, wfd_ref, g2_ref, b2_ref, o_ref):
    half = x_ref.shape[0] // 2
    g1, b1, g2, b2 = g1_ref[...], b1_ref[...], g2_ref[...], b2_ref[...]

    def mixer_out(rows):
        xb = x_ref[rows, :].astype(_BF16)
        gates = jax.nn.sigmoid(jnp.dot(xb, wg_ref[...], preferred_element_type=_F32) + bg_ref[...])
        pa = jnp.dot(oa_ref[rows, :], wa_ref[...], preferred_element_type=_F32)
        pb = jnp.dot(ob_ref[rows, :], wb_ref[...], preferred_element_type=_F32)
        mixed = (gates[:, :D_MODEL] * pa + gates[:, D_MODEL:] * pb).astype(_BF16)
        y = jnp.dot(mixed, wo_ref[...], preferred_element_type=_F32)
        return _layer_norm(DN_ALPHA * x_ref[rows, :] + y, g1, b1)

    def activations(h1):
        hb = h1.astype(_BF16)
        gt = jnp.dot(hb, wfg_ref[...], preferred_element_type=_F32)
        up = jnp.dot(hb, wfu_ref[...], preferred_element_type=_F32)
        return (jax.nn.silu(gt) * up).astype(_BF16)

    def finish(act, h1, rows):
        f = jnp.dot(act, wfd_ref[...], preferred_element_type=_F32)
        o_ref[rows, :] = _layer_norm(DN_ALPHA * h1 + f, g2, b2)

    first, second = slice(0, half), slice(half, 2 * half)
    h1_a = mixer_out(first)
    h1_b = mixer_out(second)
    finish(activations(h1_a), h1_a, first)
    finish(_after_stores(activations(h1_b), o_ref, first), h1_b, second)


def _tail(x2, oa2, ob2, consts):
    n = x2.shape[0]
    tm = TM_DENSE
    row = lambda w: pl.BlockSpec((tm, w), lambda i: (i, 0))
    return pl.pallas_call(
        _tail_kernel,
        out_shape=jax.ShapeDtypeStruct((n, D_MODEL), _F32),
        grid=(n // tm,),
        in_specs=[row(D_MODEL), row(A_Q_W), row(B_GW)] + [_const_spec(c.shape) for c in consts],
        out_specs=row(D_MODEL),
        compiler_params=pltpu.CompilerParams(
            dimension_semantics=("arbitrary",), vmem_limit_bytes=VMEM_LIMIT),
        name="tail",
    )(x2, oa2, ob2, *consts)


def _rope_tables(s):
    half = ROT_DIM // 2
    inv = np.float32(ROPE_THETA) ** (-np.arange(half, dtype=np.float32) / np.float32(half))
    ang = np.arange(s).astype(np.float32)[:, None] * inv[None, :]
    cos, sin = np.cos(ang).astype(np.float32), np.sin(ang).astype(np.float32)
    zeros = lambda w: np.zeros((s, w), np.float32)
    cos_h = np.concatenate([cos, cos, np.ones((s, HEAD_DIM - ROT_DIM), np.float32)], axis=1)
    sa_h = np.concatenate([-sin, zeros(HEAD_DIM - half)], axis=1)
    sb_h = np.concatenate([zeros(half), sin, zeros(HEAD_DIM - ROT_DIM)], axis=1)
    two = lambda a: jnp.asarray(np.concatenate([a, a], axis=1))
    return two(cos_h), two(sa_h), two(sb_h)


def _band_t(nk, halo, width):
    key = np.arange(nk)[:, None]
    rel = key - np.arange(BLOCK)[None, :]
    band = (rel >= 0) & (rel <= width)
    lead, trail = key >= halo, key < nk - halo
    variants = [band, band & lead, band & trail, band & lead & trail]
    return jnp.asarray(np.where(np.stack(variants), 0.0, NEG).astype(np.float32)).astype(_BF16)


def _ones_block_diag(nkeys):
    lo = np.broadcast_to(np.arange(LANES) < HEAD_DIM, (nkeys, LANES))
    return jnp.asarray(np.concatenate([lo, ~lo], axis=0).astype(np.float32)).astype(_BF16)


def _stacked_eye(heads):
    return jnp.asarray(np.tile(np.eye(BLOCK, dtype=np.float32), (heads, 1))).astype(_BF16)


def kernel(x, w_in, a_sink, w_gate, b_gate, w_br_a, w_br_b, w_out, ln1_g, ln1_b,
           w_ff_gate, w_ff_up, w_ff_down, ln2_g, ln2_b):
    b, s, d = x.shape
    assert d == D_MODEL and s % TS_B == 0 and s % TQ_A == 0 and s % TM_PROJ == 0
    assert w_in.shape[0] == DEPTH == 1
    cos_t, sa_t, sb_t = _rope_tables(s)
    bf = lambda w: w[0].astype(_BF16)
    row = lambda v: v[0].astype(_F32)[None, :]

    outs = _proj(x, bf(w_in), cos_t, sa_t, sb_t)
    qa, ka2, va2 = outs[:3]
    qkv = [tuple(outs[3 + 3 * g:6 + 3 * g]) for g in range(len(B_GROUPS))]
    o_a = _attn_a(qa, ka2, va2, a_sink[0].astype(_F32),
                  _band_t(3 * BLOCK, BLOCK, 2 * A_WINDOW), _stacked_eye(A_Q_HEADS // A_KV_HEADS),
                  _ones_block_diag(3 * BLOCK))
    o_b = _attn_b(qkv, _band_t(BLOCK + 2 * B_SIDE, B_SIDE, 2 * B_SIDE), _stacked_eye(2),
                  _ones_block_diag(BLOCK + 2 * B_SIDE))

    n = b * s
    out = _tail(x.reshape(n, d), o_a.reshape(n, A_Q_W), o_b.reshape(n, B_GW),
                [bf(w_gate), row(b_gate), bf(w_br_a), bf(w_br_b), bf(w_out), row(ln1_g), row(ln1_b),
                 bf(w_ff_gate), bf(w_ff_up), bf(w_ff_down), row(ln2_g), row(ln2_b)])
    return out.reshape(b, s, d)
```

```python
import functools

import numpy as np
import jax
import jax.numpy as jnp
from jax import lax
from jax.experimental import pallas as pl
from jax.experimental.pallas import tpu as pltpu

D_MODEL = 1024
HEAD_DIM = 64
ROT_DIM = HEAD_DIM // 4
ROPE_THETA = 500000.0
BLOCK = 128
A_Q_HEADS = 16
A_KV_HEADS = 4
A_WINDOW = 128
B_GROUPS = ((128, 1), (512, 4), (2048, 16))
B_HEADS_PER_GROUP = 4
B_SIDE = 64
A_Q_W = A_Q_HEADS * HEAD_DIM
A_KV_W = A_KV_HEADS * HEAD_DIM
B_GW = B_HEADS_PER_GROUP * HEAD_DIM
B_W = len(B_GROUPS) * B_GW
OFF_KA = A_Q_W
OFF_VA = OFF_KA + A_KV_W
OFF_QB = OFF_VA + A_KV_W
OFF_KB = OFF_QB + B_W
OFF_VB = OFF_KB + B_W
DEPTH = 1
DN_ALPHA = (2 * DEPTH) ** 0.25
LN_EPS = 1e-5
NEG = -1e30
LOG2E = 1.4426950408889634
LANES = 128
VMEM_LIMIT = 56 * 1024 * 1024

TM_PROJ = 1024
TQ_A = 2048
TS_B = 2048
TM_DENSE = 512
LOOKAHEAD_A = 2
LOOKAHEAD_B = 4
UNITS_B = 16
UNITS_FINAL = 8
PHASE_MAJOR_DIL = 16
SLAB_PITCH = BLOCK + 8

_BF16 = jnp.bfloat16
_F32 = jnp.float32
_CONTRACT_LAST = (((1,), (1,)), ((), ()))


def _const_spec(shape):
    nd = len(shape)
    return pl.BlockSpec(shape, lambda *_: (0,) * nd, pipeline_mode=pl.Buffered(1))


def _lane_lt64(rows):
    return lax.broadcasted_iota(jnp.int32, (rows, LANES), 1) < HEAD_DIM


def _edge_variant(tile, blk, n_tiles, n_blks):
    first = jnp.logical_and(tile == 0, blk == 0)
    last = jnp.logical_and(tile == n_tiles - 1, blk == n_blks - 1)
    return first.astype(jnp.int32) + 2 * last.astype(jnp.int32)


def _rope(t, cos, sa, sb):
    return t * cos + pltpu.roll(t, LANES - ROT_DIM // 2, 1) * sa + pltpu.roll(t, ROT_DIM // 2, 1) * sb


def _proj_kernel(x_ref, w_ref, cos_ref, sa_ref, sb_ref,
                 qa_ref, ka_ref, va_ref,
                 q0_ref, k0_ref, v0_ref, q1_ref, k1_ref, v1_ref, q2_ref, k2_ref, v2_ref,
                 slab_ref):
    tm = x_ref.shape[1]
    xb = x_ref[0].astype(_BF16)
    cos = cos_ref[...]
    sa = sa_ref[...]
    sb = sb_ref[...]
    scale = HEAD_DIM ** -0.5 * LOG2E
    lt64 = _lane_lt64(tm)

    def mm(c0, width=2 * LANES):
        return jnp.dot(xb, w_ref[:, c0:c0 + width], preferred_element_type=_F32)

    def mixer_a_q(c):
        r = mm(c * 2 * LANES)
        for h in range(2):
            t = _rope(r[:, h * LANES:(h + 1) * LANES], cos, sa, sb) * scale
            qa_ref[0, :, c * 2 * LANES + h * LANES:c * 2 * LANES + (h + 1) * LANES] = t.astype(_BF16)

    def mixer_a_kv(off, out_ref, rot):
        r = mm(off)
        for h in range(2):
            t = r[:, h * LANES:(h + 1) * LANES]
            if rot:
                t = _rope(t, cos, sa, sb)
            sw = pltpu.roll(t, HEAD_DIM, 1)
            out_ref[0, :, (2 * h) * LANES:(2 * h + 1) * LANES] = jnp.where(lt64, t, sw).astype(_BF16)
            out_ref[0, :, (2 * h + 1) * LANES:(2 * h + 2) * LANES] = jnp.where(lt64, sw, t).astype(_BF16)

    outs = ((q0_ref, k0_ref, v0_ref), (q1_ref, k1_ref, v1_ref), (q2_ref, k2_ref, v2_ref))

    def mixer_b(g, kind):
        dil = B_GROUPS[g][1]
        out_ref = outs[g][kind]
        slab = 2 * (3 * (g - 1) + kind)
        r = mm((OFF_QB, OFF_KB, OFF_VB)[kind] + g * B_GW)
        for h in range(2):
            t = r[:, h * LANES:(h + 1) * LANES]
            if kind < 2:
                t = _rope(t, cos, sa, sb)
            if kind == 0:
                t = t * scale
            if dil == 1:
                out_ref[0, 0, :, h * LANES:(h + 1) * LANES] = t.astype(_BF16)
            elif dil % 16:
                slab_ref[slab + h, 0:tm] = t
            else:
                for k in range(tm // dil):
                    slab_ref[slab + h, k * (dil + 8):k * (dil + 8) + dil] = t[k * dil:(k + 1) * dil]
        if dil > 1:
            rows = tm // dil
            pitch = dil if dil % 16 else dil + 8
            for h in range(2):
                for p in range(dil):
                    v = slab_ref[slab + h, pl.ds(p, rows, stride=pitch), :]
                    out_ref[0, p, :, h * LANES:(h + 1) * LANES] = v.astype(_BF16)

    for c in range(A_Q_W // (2 * LANES)):
        mixer_a_q(c)
    mixer_a_kv(OFF_KA, ka_ref, True)
    mixer_a_kv(OFF_VA, va_ref, False)
    for g in range(len(B_GROUPS)):
        for kind in range(3):
            mixer_b(g, kind)


def _proj(x, w_in_bf, cos_t, sa_t, sb_t):
    b, s, _ = x.shape
    tm = TM_PROJ
    grid = (s // tm, b)
    tok = lambda i, j: (j, i, 0)
    tab = pl.BlockSpec((tm, LANES), lambda i, j: (i, 0))
    out_shape = [jax.ShapeDtypeStruct((b, s, A_Q_W), _BF16),
                 jax.ShapeDtypeStruct((b, s, 2 * A_KV_W), _BF16),
                 jax.ShapeDtypeStruct((b, s, 2 * A_KV_W), _BF16)]
    out_specs = [pl.BlockSpec((1, tm, A_Q_W), tok),
                 pl.BlockSpec((1, tm, 2 * A_KV_W), tok),
                 pl.BlockSpec((1, tm, 2 * A_KV_W), tok)]
    for _, dil in B_GROUPS:
        for _ in range(3):
            out_shape.append(jax.ShapeDtypeStruct((b, dil, s // dil, B_GW), _BF16))
            out_specs.append(pl.BlockSpec((1, dil, tm // dil, B_GW), lambda i, j: (j, 0, i, 0)))
    return pl.pallas_call(
        _proj_kernel,
        out_shape=out_shape,
        grid=grid,
        in_specs=[pl.BlockSpec((1, tm, D_MODEL), tok), _const_spec(w_in_bf.shape), tab, tab, tab],
        out_specs=out_specs,
        scratch_shapes=[pltpu.VMEM((2 * 3 * (len(B_GROUPS) - 1), tm + tm // 2, LANES), _F32)],
        compiler_params=pltpu.CompilerParams(
            dimension_semantics=("arbitrary", "arbitrary"), vmem_limit_bytes=VMEM_LIMIT),
        name="proj",
    )(x, w_in_bf, cos_t, sa_t, sb_t)


def _attn_a_kernel(sink_ref, q_ref, kp_ref, kc_ref, kn_ref, vp_ref, vc_ref, vn_ref, bandt_ref, eye_ref,
                   ones_ref, o_ref):
    tq = q_ref.shape[1]
    t = pl.program_id(1)
    nblk = tq // BLOCK
    nkeys = 3 * BLOCK
    grp = A_Q_HEADS // A_KV_HEADS
    lt64_q = _lane_lt64(BLOCK)
    lt64_k = _lane_lt64(nkeys)
    zq = jnp.zeros((BLOCK, LANES), _BF16)
    zk = jnp.zeros((nkeys, LANES), _BF16)
    eye = eye_ref[...]
    ones_bd = ones_ref[...]

    def window(prev_ref, cur_ref, next_ref, j, k):
        cols = slice(k * LANES, (k + 1) * LANES)
        pieces = []
        if j == 0:
            pieces.append(prev_ref[0, :, cols])
        pieces.append(cur_ref[0, max(j - 1, 0) * BLOCK:min(j + 2, nblk) * BLOCK, cols])
        if j == nblk - 1:
            pieces.append(next_ref[0, :, cols])
        return pieces[0] if len(pieces) == 1 else jnp.concatenate(pieces, axis=0)

    def scores(j, k):
        rows = slice(j * BLOCK, (j + 1) * BLOCK)
        bandt = bandt_ref[_edge_variant(t, j, pl.num_programs(1), nblk)]
        kd = window(kp_ref, kc_ref, kn_ref, j, k)
        qs = []
        for pr in range(2):
            qp = q_ref[0, rows, (2 * k + pr) * LANES:(2 * k + pr + 1) * LANES]
            qs.append(jnp.where(lt64_q, qp, zq))
            qs.append(jnp.where(lt64_q, zq, qp))
        lhs = jnp.concatenate([jnp.concatenate(qs, axis=0), eye], axis=1)
        rhs = jnp.concatenate([kd, bandt], axis=1)
        return lax.dot_general(lhs, rhs, _CONTRACT_LAST, preferred_element_type=_F32)

    def finish(sc, j, k):
        ps, sinks = [], []
        for i in range(grp):
            si = sc[i * BLOCK:(i + 1) * BLOCK]
            sk = sink_ref[grp * k + i] * LOG2E
            m = jnp.maximum(jnp.max(si, axis=1, keepdims=True), sk)
            ps.append(jnp.exp2((si - m).astype(_BF16)))
            sinks.append(jnp.exp2(sk - m))
        pst = jnp.concatenate([jnp.concatenate([ps[0], ps[1]], axis=1),
                               jnp.concatenate([ps[2], ps[3]], axis=1)], axis=0)
        vd = window(vp_ref, vc_ref, vn_ref, j, k)
        vbd = jnp.concatenate([jnp.where(lt64_k, vd, zk), jnp.where(lt64_k, zk, vd)], axis=0)
        ol = jnp.dot(pst, jnp.concatenate([vbd, ones_bd], axis=1), preferred_element_type=_F32)
        for pr in range(2):
            rows = slice(pr * BLOCK, (pr + 1) * BLOCK)
            den = ol[rows, LANES:] + jnp.where(lt64_q, sinks[2 * pr], sinks[2 * pr + 1])
            res = ol[rows, :LANES] / den
            o_ref[0, j * BLOCK:(j + 1) * BLOCK, (2 * k + pr) * LANES:(2 * k + pr + 1) * LANES] = res.astype(o_ref.dtype)

    work = [(j, k) for j in range(nblk) for k in range(A_KV_HEADS)]
    scs = []
    for i in range(len(work) + LOOKAHEAD_A):
        if i < len(work):
            scs.append(scores(*work[i]))
        if i >= LOOKAHEAD_A:
            finish(scs[i - LOOKAHEAD_A], *work[i - LOOKAHEAD_A])


def _attn_a(qa, ka2, va2, sink, bandt, eye, ones_bd):
    b, s, _ = qa.shape
    tq = TQ_A
    nb = tq // BLOCK
    last = s // BLOCK - 1
    cur = lambda i, j: (i, j, 0)
    prev = lambda i, j: (i, jnp.maximum(j * nb - 1, 0), 0)
    nxt = lambda i, j: (i, jnp.minimum((j + 1) * nb, last), 0)
    kvw = ka2.shape[-1]
    kv_specs = [pl.BlockSpec((1, BLOCK, kvw), prev), pl.BlockSpec((1, tq, kvw), cur),
                pl.BlockSpec((1, BLOCK, kvw), nxt)]
    return pl.pallas_call(
        _attn_a_kernel,
        out_shape=jax.ShapeDtypeStruct((b, s, A_Q_W), _BF16),
        grid=(b, s // tq),
        in_specs=[pl.BlockSpec(memory_space=pltpu.SMEM),
                  pl.BlockSpec((1, tq, A_Q_W), cur)] + kv_specs + kv_specs
                 + [_const_spec(bandt.shape), _const_spec(eye.shape), _const_spec(ones_bd.shape)],
        out_specs=pl.BlockSpec((1, tq, A_Q_W), cur),
        compiler_params=pltpu.CompilerParams(
            dimension_semantics=("arbitrary", "arbitrary"), vmem_limit_bytes=VMEM_LIMIT),
        name="attn_a",
    )(sink, qa, ka2, ka2, ka2, va2, va2, va2, bandt, eye, ones_bd)


_SLAB_O, _SLAB_M, _SLAB_L = 0, 1, 2


def _halo_window(prev_ref, cur_ref, next_ref, p, j, cols):
    nt = cur_ref.shape[2]
    lo, hi = j * BLOCK - B_SIDE, (j + 1) * BLOCK + B_SIDE
    pieces = []
    if lo < 0:
        pieces.append(prev_ref[0, p, :, cols])
    pieces.append(cur_ref[0, p, max(lo, 0):min(hi, nt), cols])
    if hi > nt:
        pieces.append(next_ref[0, p, :, cols])
    return pieces[0] if len(pieces) == 1 else jnp.concatenate(pieces, axis=0)


def _attn_b_kernel(*refs):
    ng = len(B_GROUPS)
    in_refs = refs[:7 * ng]
    bandt_ref, eye_ref, ones_ref, o_ref = refs[7 * ng:7 * ng + 4]
    slab_refs = (None,) + tuple(refs[7 * ng + 4:])
    ts = o_ref.shape[1]
    t = pl.program_id(1)
    nkeys = BLOCK + 2 * B_SIDE
    lt64_q = _lane_lt64(BLOCK)
    lt64_k = _lane_lt64(nkeys)
    zq = jnp.zeros((BLOCK, LANES), _BF16)
    zk = jnp.zeros((nkeys, LANES), _BF16)
    eye = eye_ref[...]
    ones_bd = ones_ref[...]

    def units(g, pjs):
        q_ref, kp_ref, kc_ref, kn_ref, vp_ref, vc_ref, vn_ref = in_refs[7 * g:7 * g + 7]
        nj = ts // B_GROUPS[g][1] // BLOCK
        work = [(p, j, slice(pr * LANES, (pr + 1) * LANES)) for p, j in pjs for pr in range(2)]

        def scores(p, j, cols):
            bandt = bandt_ref[_edge_variant(t, j, pl.num_programs(1), nj)]
            qp = q_ref[0, p, j * BLOCK:(j + 1) * BLOCK, cols]
            kp = _halo_window(kp_ref, kc_ref, kn_ref, p, j, cols)
            qst = jnp.concatenate([jnp.where(lt64_q, qp, zq), jnp.where(lt64_q, zq, qp)], axis=0)
            lhs = jnp.concatenate([qst, eye], axis=1)
            rhs = jnp.concatenate([kp, bandt], axis=1)
            return lax.dot_general(lhs, rhs, _CONTRACT_LAST, preferred_element_type=_F32)

        def finish(sc, p, j, cols):
            m = jnp.max(sc, axis=1, keepdims=True)
            eb = jnp.exp2((sc - m).astype(_BF16))
            pcat = jnp.concatenate([eb[:BLOCK], eb[BLOCK:]], axis=1)
            vp = _halo_window(vp_ref, vc_ref, vn_ref, p, j, cols)
            vbd = jnp.concatenate([jnp.where(lt64_k, vp, zk), jnp.where(lt64_k, zk, vp)], axis=0)
            ol = jnp.dot(pcat, jnp.concatenate([vbd, ones_bd], axis=1), preferred_element_type=_F32)
            return ol[:, :LANES], jnp.where(lt64_q, m[:BLOCK], m[BLOCK:]), ol[:, LANES:]

        scs, res = [], []
        for i in range(len(work) + LOOKAHEAD_B):
            if i < len(work):
                scs.append(scores(*work[i]))
            if i >= LOOKAHEAD_B:
                res.append(finish(scs[i - LOOKAHEAD_B], *work[i - LOOKAHEAD_B]))
        return [res[2 * i:2 * i + 2] for i in range(len(pjs))]

    for g in range(1, ng):
        dil = B_GROUPS[g][1]
        nj = ts // dil // BLOCK
        for u0 in range(0, dil * nj, UNITS_B):
            pjs = [((u0 + i) // nj, (u0 + i) % nj) for i in range(UNITS_B)]
            for (p, j), res in zip(pjs, units(g, pjs)):
                if dil == PHASE_MAJOR_DIL:
                    rows = pl.ds(p * SLAB_PITCH, BLOCK)
                else:
                    rows = pl.ds(p + j * (BLOCK * dil), BLOCK, stride=dil)
                for pr in range(2):
                    for kind in range(3):
                        slab_refs[g][kind * 2 + pr, rows, :] = res[pr][kind]

    def slab_block(g, idx, j):
        dil = B_GROUPS[g][1]
        if dil != PHASE_MAJOR_DIL:
            return slab_refs[g][idx, j * BLOCK:(j + 1) * BLOCK, :]
        per = BLOCK // dil
        pieces = [slab_refs[g][idx, pl.ds((8 * (r % 2)) * SLAB_PITCH + j * per + r // 2, 8, stride=SLAB_PITCH), :]
                  for r in range(BLOCK // 8)]
        return jnp.concatenate(pieces, axis=0)

    for j0 in range(0, ts // BLOCK, UNITS_FINAL):
        pjs = [(0, j0 + i) for i in range(UNITS_FINAL)]
        for (_, j), res in zip(pjs, units(0, pjs)):
            for pr in range(2):
                os_, ms_, ls_ = [res[pr][0]], [res[pr][1]], [res[pr][2]]
                for g in range(1, ng):
                    os_.append(slab_block(g, _SLAB_O * 2 + pr, j))
                    ms_.append(slab_block(g, _SLAB_M * 2 + pr, j))
                    ls_.append(slab_block(g, _SLAB_L * 2 + pr, j))
                mx = functools.reduce(jnp.maximum, ms_)
                num = 0.0
                den = 0.0
                for o, m, l in zip(os_, ms_, ls_):
                    a = jnp.exp2(m - mx)
                    num = num + a * o
                    den = den + a * l
                o_ref[0, j * BLOCK:(j + 1) * BLOCK, pr * LANES:(pr + 1) * LANES] = (num / den).astype(o_ref.dtype)


def _attn_b(qkv, bandt, eye, ones_bd):
    b = qkv[0][0].shape[0]
    s = qkv[0][0].shape[2]
    ts = TS_B
    nt_tiles = s // ts
    args, in_specs = [], []
    for (q, k, v), (_, dil) in zip(qkv, B_GROUPS):
        nt = ts // dil
        nh = nt // B_SIDE
        last = s // dil // B_SIDE - 1
        cur = pl.BlockSpec((1, dil, nt, B_GW), lambda i, j: (i, 0, j, 0))
        prev = pl.BlockSpec((1, dil, B_SIDE, B_GW), lambda i, j, nh=nh: (i, 0, jnp.maximum(j * nh - 1, 0), 0))
        nxt = pl.BlockSpec((1, dil, B_SIDE, B_GW),
                           lambda i, j, nh=nh, last=last: (i, 0, jnp.minimum((j + 1) * nh, last), 0))
        args += [q, k, k, k, v, v, v]
        in_specs += [cur, prev, cur, nxt, prev, cur, nxt]
    slab_shapes = []
    for _, dil in B_GROUPS[1:]:
        rows = dil * SLAB_PITCH if dil == PHASE_MAJOR_DIL else ts
        slab_shapes.append(pltpu.VMEM((3 * 2, rows, LANES), _F32))
    return pl.pallas_call(
        _attn_b_kernel,
        out_shape=jax.ShapeDtypeStruct((b, s, B_GW), _BF16),
        grid=(b, nt_tiles),
        in_specs=in_specs + [_const_spec(bandt.shape), _const_spec(eye.shape), _const_spec(ones_bd.shape)],
        out_specs=pl.BlockSpec((1, ts, B_GW), lambda i, j: (i, j, 0)),
        scratch_shapes=slab_shapes,
        compiler_params=pltpu.CompilerParams(
            dimension_semantics=("arbitrary", "arbitrary"), vmem_limit_bytes=VMEM_LIMIT),
        name="attn_b",
    )(*args, bandt, eye, ones_bd)


def _layer_norm(v, g, b):
    mu = jnp.mean(v, axis=-1, keepdims=True)
    d = v - mu
    var = jnp.mean(d * d, axis=-1, keepdims=True)
    return d * lax.rsqrt(var + LN_EPS) * g + b


def _after_stores(val, ref, rows):
    probe = jnp.sum(ref[rows, 0:LANES].reshape(-1, 8, LANES), axis=0)
    n = 8 * (4 // val.dtype.itemsize)
    probe = jnp.concatenate([probe] * (n // 8), axis=0).astype(val.dtype)
    head = jnp.where(pl.program_id(0) < 0, probe, val[0:n, 0:LANES])
    top = jnp.concatenate([head, val[0:n, LANES:]], axis=1)
    return jnp.concatenate([top, val[n:]], axis=0)


def _tail_kernel(x_ref, oa_ref, ob_ref, wg_ref, bg_ref, wa_ref, wb_ref, wo_ref, g1_ref, b1_ref,
                 wfg_ref, wfu_ref, wfd_ref, g2_ref, b2_ref, o_ref):
    half = x_ref.shape[0] // 2
    g1, b1, g2, b2 = g1_ref[...], b1_ref[...], g2_ref[...], b2_ref[...]

    def mixer_out(rows):
        xb = x_ref[rows, :].astype(_BF16)
        gates = jax.nn.sigmoid(jnp.dot(xb, wg_ref[...], preferred_element_type=_F32) + bg_ref[...])
        pa = jnp.dot(oa_ref[rows, :], wa_ref[...], preferred_element_type=_F32)
        pb = jnp.dot(ob_ref[rows, :], wb_ref[...], preferred_element_type=_F32)
        mixed = (gates[:, :D_MODEL] * pa + gates[:, D_MODEL:] * pb).astype(_BF16)
        y = jnp.dot(mixed, wo_ref[...], preferred_element_type=_F32)
        return _layer_norm(DN_ALPHA * x_ref[rows, :] + y, g1, b1)

    def activations(h1):
        hb = h1.astype(_BF16)
        gt = jnp.dot(hb, wfg_ref[...], preferred_element_type=_F32)
        up = jnp.dot(hb, wfu_ref[...], preferred_element_type=_F32)
        return (jax.nn.silu(gt) * up).astype(_BF16)

    def finish(act, h1, rows):
        f = jnp.dot(act, wfd_ref[...], preferred_element_type=_F32)
        o_ref[rows, :] = _layer_norm(DN_ALPHA * h1 + f, g2, b2)

    first, second = slice(0, half), slice(half, 2 * half)
    h1_a = mixer_out(first)
    h1_b = mixer_out(second)
    finish(activations(h1_a), h1_a, first)
    finish(_after_stores(activations(h1_b), o_ref, first), h1_b, second)


def _tail(x2, oa2, ob2, consts):
    n = x2.shape[0]
    tm = TM_DENSE
    row = lambda w: pl.BlockSpec((tm, w), lambda i: (i, 0))
    return pl.pallas_call(
        _tail_kernel,
        out_shape=jax.ShapeDtypeStruct((n, D_MODEL), _F32),
        grid=(n // tm,),
        in_specs=[row(D_MODEL), row(A_Q_W), row(B_GW)] + [_const_spec(c.shape) for c in consts],
        out_specs=row(D_MODEL),
        compiler_params=pltpu.CompilerParams(
            dimension_semantics=("arbitrary",), vmem_limit_bytes=VMEM_LIMIT),
        name="tail",
    )(x2, oa2, ob2, *consts)


def _rope_tables(s):
    half = ROT_DIM // 2
    inv = np.float32(ROPE_THETA) ** (-np.arange(half, dtype=np.float32) / np.float32(half))
    ang = np.arange(s).astype(np.float32)[:, None] * inv[None, :]
    cos, sin = np.cos(ang).astype(np.float32), np.sin(ang).astype(np.float32)
    zeros = lambda w: np.zeros((s, w), np.float32)
    cos_h = np.concatenate([cos, cos, np.ones((s, HEAD_DIM - ROT_DIM), np.float32)], axis=1)
    sa_h = np.concatenate([-sin, zeros(HEAD_DIM - half)], axis=1)
    sb_h = np.concatenate([zeros(half), sin, zeros(HEAD_DIM - ROT_DIM)], axis=1)
    two = lambda a: jnp.asarray(np.concatenate([a, a], axis=1))
    return two(cos_h), two(sa_h), two(sb_h)


def _band_t(nk, halo, width):
    key = np.arange(nk)[:, None]
    rel = key - np.arange(BLOCK)[None, :]
    band = (rel >= 0) & (rel <= width)
    lead, trail = key >= halo, key < nk - halo
    variants = [band, band & lead, band & trail, band & lead & trail]
    return jnp.asarray(np.where(np.stack(variants), 0.0, NEG).astype(np.float32)).astype(_BF16)


def _ones_block_diag(nkeys):
    lo = np.broadcast_to(np.arange(LANES) < HEAD_DIM, (nkeys, LANES))
    return jnp.asarray(np.concatenate([lo, ~lo], axis=0).astype(np.float32)).astype(_BF16)


def _stacked_eye(heads):
    return jnp.asarray(np.tile(np.eye(BLOCK, dtype=np.float32), (heads, 1))).astype(_BF16)


def kernel(x, w_in, a_sink, w_gate, b_gate, w_br_a, w_br_b, w_out, ln1_g, ln1_b,
           w_ff_gate, w_ff_up, w_ff_down, ln2_g, ln2_b):
    b, s, d = x.shape
    assert d == D_MODEL and s % TS_B == 0 and s % TQ_A == 0 and s % TM_PROJ == 0
    assert w_in.shape[0] == DEPTH == 1
    cos_t, sa_t, sb_t = _rope_tables(s)
    bf = lambda w: w[0].astype(_BF16)
    row = lambda v: v[0].astype(_F32)[None, :]

    outs = _proj(x, bf(w_in), cos_t, sa_t, sb_t)
    qa, ka2, va2 = outs[:3]
    qkv = [tuple(outs[3 + 3 * g:6 + 3 * g]) for g in range(len(B_GROUPS))]
    o_a = _attn_a(qa, ka2, va2, a_sink[0].astype(_F32),
                  _band_t(3 * BLOCK, BLOCK, 2 * A_WINDOW), _stacked_eye(A_Q_HEADS // A_KV_HEADS),
                  _ones_block_diag(3 * BLOCK))
    o_b = _attn_b(qkv, _band_t(BLOCK + 2 * B_SIDE, B_SIDE, 2 * B_SIDE), _stacked_eye(2),
                  _ones_block_diag(BLOCK + 2 * B_SIDE))

    n = b * s
    out = _tail(x.reshape(n, d), o_a.reshape(n, A_Q_W), o_b.reshape(n, B_GW),
                [bf(w_gate), row(b_gate), bf(w_br_a), bf(w_br_b), bf(w_out), row(ln1_g), row(ln1_b),
                 bf(w_ff_gate), bf(w_ff_up), bf(w_ff_down), row(ln2_g), row(ln2_b)])
    return out.reshape(b, s, d)
```

```python
import functools

import numpy as np
import jax
import jax.numpy as jnp
from jax import lax
from jax.experimental import pallas as pl
from jax.experimental.pallas import tpu as pltpu

D_MODEL = 1024
HEAD_DIM = 64
ROT_DIM = HEAD_DIM // 4
ROPE_THETA = 500000.0
BLOCK = 128
A_Q_HEADS = 16
A_KV_HEADS = 4
A_WINDOW = 128
B_GROUPS = ((128, 1), (512, 4), (2048, 16))
B_HEADS_PER_GROUP = 4
B_SIDE = 64
A_Q_W = A_Q_HEADS * HEAD_DIM
A_KV_W = A_KV_HEADS * HEAD_DIM
B_GW = B_HEADS_PER_GROUP * HEAD_DIM
B_W = len(B_GROUPS) * B_GW
OFF_KA = A_Q_W
OFF_VA = OFF_KA + A_KV_W
OFF_QB = OFF_VA + A_KV_W
OFF_KB = OFF_QB + B_W
OFF_VB = OFF_KB + B_W
DEPTH = 1
DN_ALPHA = (2 * DEPTH) ** 0.25
LN_EPS = 1e-5
NEG = -1e30
LOG2E = 1.4426950408889634
LANES = 128
SUBLANES = 8
WORD_BYTES = 4
VMEM_LIMIT = 56 * 1024 * 1024

TM_PROJ = 1024
TQ_A = 2048
TS_B = 2048
TM_DENSE = 512
LOOKAHEAD_A = 2
LOOKAHEAD_B = 4
UNITS_B = 16
UNITS_FINAL = 8
PHASE_MAJOR_DIL = 16
SLAB_PITCH = BLOCK + SUBLANES

_BF16 = jnp.bfloat16
_F32 = jnp.float32
_CONTRACT_LAST = (((1,), (1,)), ((), ()))


def _const_spec(shape):
    nd = len(shape)
    return pl.BlockSpec(shape, lambda *_: (0,) * nd, pipeline_mode=pl.Buffered(1))


def _lane_lt64(rows):
    return lax.broadcasted_iota(jnp.int32, (rows, LANES), 1) < HEAD_DIM


def _edge_variant(tile, blk, n_tiles, n_blks):
    first = jnp.logical_and(tile == 0, blk == 0)
    last = jnp.logical_and(tile == n_tiles - 1, blk == n_blks - 1)
    return first.astype(jnp.int32) + 2 * last.astype(jnp.int32)


def _rope(t, cos, sa, sb):
    return t * cos + pltpu.roll(t, LANES - ROT_DIM // 2, 1) * sa + pltpu.roll(t, ROT_DIM // 2, 1) * sb


def _proj_kernel(x_ref, w_ref, cos_ref, sa_ref, sb_ref,
                 qa_ref, ka_ref, va_ref,
                 q0_ref, k0_ref, v0_ref, q1_ref, k1_ref, v1_ref, q2_ref, k2_ref, v2_ref,
                 slab_ref):
    tm = x_ref.shape[1]
    xb = x_ref[0].astype(_BF16)
    cos = cos_ref[...]
    sa = sa_ref[...]
    sb = sb_ref[...]
    scale = HEAD_DIM ** -0.5 * LOG2E
    lt64 = _lane_lt64(tm)

    def mm(c0, width=2 * LANES):
        return jnp.dot(xb, w_ref[:, c0:c0 + width], preferred_element_type=_F32)

    def mixer_a_q(c):
        r = mm(c * 2 * LANES)
        for h in range(2):
            t = _rope(r[:, h * LANES:(h + 1) * LANES], cos, sa, sb) * scale
            qa_ref[0, :, c * 2 * LANES + h * LANES:c * 2 * LANES + (h + 1) * LANES] = t.astype(_BF16)

    def mixer_a_kv(off, out_ref, rot):
        r = mm(off)
        for h in range(2):
            t = r[:, h * LANES:(h + 1) * LANES]
            if rot:
                t = _rope(t, cos, sa, sb)
            sw = pltpu.roll(t, HEAD_DIM, 1)
            out_ref[0, :, (2 * h) * LANES:(2 * h + 1) * LANES] = jnp.where(lt64, t, sw).astype(_BF16)
            out_ref[0, :, (2 * h + 1) * LANES:(2 * h + 2) * LANES] = jnp.where(lt64, sw, t).astype(_BF16)

    outs = ((q0_ref, k0_ref, v0_ref), (q1_ref, k1_ref, v1_ref), (q2_ref, k2_ref, v2_ref))

    def mixer_b(g, kind):
        dil = B_GROUPS[g][1]
        out_ref = outs[g][kind]
        slab = 2 * (3 * (g - 1) + kind)
        r = mm((OFF_QB, OFF_KB, OFF_VB)[kind] + g * B_GW)
        for h in range(2):
            t = r[:, h * LANES:(h + 1) * LANES]
            if kind < 2:
                t = _rope(t, cos, sa, sb)
            if kind == 0:
                t = t * scale
            if dil == 1:
                out_ref[0, 0, :, h * LANES:(h + 1) * LANES] = t.astype(_BF16)
            elif dil % 16:
                slab_ref[slab + h, 0:tm] = t
            else:
                for k in range(tm // dil):
                    slab_ref[slab + h, k * (dil + SUBLANES):k * (dil + SUBLANES) + dil] = t[k * dil:(k + 1) * dil]
        if dil > 1:
            rows = tm // dil
            pitch = dil if dil % 16 else dil + SUBLANES
            for h in range(2):
                for p in range(dil):
                    v = slab_ref[slab + h, pl.ds(p, rows, stride=pitch), :]
                    out_ref[0, p, :, h * LANES:(h + 1) * LANES] = v.astype(_BF16)

    for c in range(A_Q_W // (2 * LANES)):
        mixer_a_q(c)
    mixer_a_kv(OFF_KA, ka_ref, True)
    mixer_a_kv(OFF_VA, va_ref, False)
    for g in range(len(B_GROUPS)):
        for kind in range(3):
            mixer_b(g, kind)


def _proj(x, w_in_bf, cos_t, sa_t, sb_t):
    b, s, _ = x.shape
    tm = TM_PROJ
    grid = (s // tm, b)
    tok = lambda i, j: (j, i, 0)
    tab = pl.BlockSpec((tm, LANES), lambda i, j: (i, 0))
    out_shape = [jax.ShapeDtypeStruct((b, s, A_Q_W), _BF16),
                 jax.ShapeDtypeStruct((b, s, 2 * A_KV_W), _BF16),
                 jax.ShapeDtypeStruct((b, s, 2 * A_KV_W), _BF16)]
    out_specs = [pl.BlockSpec((1, tm, A_Q_W), tok),
                 pl.BlockSpec((1, tm, 2 * A_KV_W), tok),
                 pl.BlockSpec((1, tm, 2 * A_KV_W), tok)]
    for _, dil in B_GROUPS:
        for _ in range(3):
            out_shape.append(jax.ShapeDtypeStruct((b, dil, s // dil, B_GW), _BF16))
            out_specs.append(pl.BlockSpec((1, dil, tm // dil, B_GW), lambda i, j: (j, 0, i, 0)))
    return pl.pallas_call(
        _proj_kernel,
        out_shape=out_shape,
        grid=grid,
        in_specs=[pl.BlockSpec((1, tm, D_MODEL), tok), _const_spec(w_in_bf.shape), tab, tab, tab],
        out_specs=out_specs,
        scratch_shapes=[pltpu.VMEM((2 * 3 * (len(B_GROUPS) - 1), tm + tm // 2, LANES), _F32)],
        compiler_params=pltpu.CompilerParams(
            dimension_semantics=("arbitrary", "arbitrary"), vmem_limit_bytes=VMEM_LIMIT),
        name="proj",
    )(x, w_in_bf, cos_t, sa_t, sb_t)


def _attn_a_kernel(sink_ref, q_ref, kp_ref, kc_ref, kn_ref, vp_ref, vc_ref, vn_ref, bandt_ref, eye_ref,
                   ones_ref, o_ref):
    tq = q_ref.shape[1]
    t = pl.program_id(1)
    nblk = tq // BLOCK
    nkeys = 3 * BLOCK
    grp = A_Q_HEADS // A_KV_HEADS
    lt64_q = _lane_lt64(BLOCK)
    lt64_k = _lane_lt64(nkeys)
    zq = jnp.zeros((BLOCK, LANES), _BF16)
    zk = jnp.zeros((nkeys, LANES), _BF16)
    eye = eye_ref[...]
    ones_bd = ones_ref[...]

    def window(prev_ref, cur_ref, next_ref, j, k):
        cols = slice(k * LANES, (k + 1) * LANES)
        pieces = []
        if j == 0:
            pieces.append(prev_ref[0, :, cols])
        pieces.append(cur_ref[0, max(j - 1, 0) * BLOCK:min(j + 2, nblk) * BLOCK, cols])
        if j == nblk - 1:
            pieces.append(next_ref[0, :, cols])
        return pieces[0] if len(pieces) == 1 else jnp.concatenate(pieces, axis=0)

    def scores(j, k):
        rows = slice(j * BLOCK, (j + 1) * BLOCK)
        bandt = bandt_ref[_edge_variant(t, j, pl.num_programs(1), nblk)]
        kd = window(kp_ref, kc_ref, kn_ref, j, k)
        qs = []
        for pr in range(2):
            qp = q_ref[0, rows, (2 * k + pr) * LANES:(2 * k + pr + 1) * LANES]
            qs.append(jnp.where(lt64_q, qp, zq))
            qs.append(jnp.where(lt64_q, zq, qp))
        lhs = jnp.concatenate([jnp.concatenate(qs, axis=0), eye], axis=1)
        rhs = jnp.concatenate([kd, bandt], axis=1)
        return lax.dot_general(lhs, rhs, _CONTRACT_LAST, preferred_element_type=_F32)

    def finish(sc, j, k):
        ps, sinks = [], []
        for i in range(grp):
            si = sc[i * BLOCK:(i + 1) * BLOCK]
            sk = sink_ref[grp * k + i] * LOG2E
            m = jnp.maximum(jnp.max(si, axis=1, keepdims=True), sk)
            ps.append(jnp.exp2((si - m).astype(_BF16)))
            sinks.append(jnp.exp2(sk - m))
        pst = jnp.concatenate([jnp.concatenate([ps[0], ps[1]], axis=1),
                               jnp.concatenate([ps[2], ps[3]], axis=1)], axis=0)
        vd = window(vp_ref, vc_ref, vn_ref, j, k)
        vbd = jnp.concatenate([jnp.where(lt64_k, vd, zk), jnp.where(lt64_k, zk, vd)], axis=0)
        ol = jnp.dot(pst, jnp.concatenate([vbd, ones_bd], axis=1), preferred_element_type=_F32)
        for pr in range(2):
            rows = slice(pr * BLOCK, (pr + 1) * BLOCK)
            den = ol[rows, LANES:] + jnp.where(lt64_q, sinks[2 * pr], sinks[2 * pr + 1])
            res = ol[rows, :LANES] / den
            o_ref[0, j * BLOCK:(j + 1) * BLOCK, (2 * k + pr) * LANES:(2 * k + pr + 1) * LANES] = res.astype(o_ref.dtype)

    work = [(j, k) for j in range(nblk) for k in range(A_KV_HEADS)]
    scs = []
    for i in range(len(work) + LOOKAHEAD_A):
        if i < len(work):
            scs.append(scores(*work[i]))
        if i >= LOOKAHEAD_A:
            finish(scs[i - LOOKAHEAD_A], *work[i - LOOKAHEAD_A])


def _attn_a(qa, ka2, va2, sink, bandt, eye, ones_bd):
    b, s, _ = qa.shape
    tq = TQ_A
    nb = tq // BLOCK
    last = s // BLOCK - 1
    cur = lambda i, j: (i, j, 0)
    prev = lambda i, j: (i, jnp.maximum(j * nb - 1, 0), 0)
    nxt = lambda i, j: (i, jnp.minimum((j + 1) * nb, last), 0)
    kvw = ka2.shape[-1]
    kv_specs = [pl.BlockSpec((1, BLOCK, kvw), prev), pl.BlockSpec((1, tq, kvw), cur),
                pl.BlockSpec((1, BLOCK, kvw), nxt)]
    return pl.pallas_call(
        _attn_a_kernel,
        out_shape=jax.ShapeDtypeStruct((b, s, A_Q_W), _BF16),
        grid=(b, s // tq),
        in_specs=[pl.BlockSpec(memory_space=pltpu.SMEM),
                  pl.BlockSpec((1, tq, A_Q_W), cur)] + kv_specs + kv_specs
                 + [_const_spec(bandt.shape), _const_spec(eye.shape), _const_spec(ones_bd.shape)],
        out_specs=pl.BlockSpec((1, tq, A_Q_W), cur),
        compiler_params=pltpu.CompilerParams(
            dimension_semantics=("arbitrary", "arbitrary"), vmem_limit_bytes=VMEM_LIMIT),
        name="attn_a",
    )(sink, qa, ka2, ka2, ka2, va2, va2, va2, bandt, eye, ones_bd)


_SLAB_O, _SLAB_M, _SLAB_L = 0, 1, 2


def _halo_window(prev_ref, cur_ref, next_ref, p, j, cols):
    nt = cur_ref.shape[2]
    lo, hi = j * BLOCK - B_SIDE, (j + 1) * BLOCK + B_SIDE
    pieces = []
    if lo < 0:
        pieces.append(prev_ref[0, p, :, cols])
    pieces.append(cur_ref[0, p, max(lo, 0):min(hi, nt), cols])
    if hi > nt:
        pieces.append(next_ref[0, p, :, cols])
    return pieces[0] if len(pieces) == 1 else jnp.concatenate(pieces, axis=0)


def _attn_b_kernel(*refs):
    ng = len(B_GROUPS)
    in_refs = refs[:7 * ng]
    bandt_ref, eye_ref, ones_ref, o_ref = refs[7 * ng:7 * ng + 4]
    slab_refs = (None,) + tuple(refs[7 * ng + 4:])
    ts = o_ref.shape[1]
    t = pl.program_id(1)
    nkeys = BLOCK + 2 * B_SIDE
    lt64_q = _lane_lt64(BLOCK)
    lt64_k = _lane_lt64(nkeys)
    zq = jnp.zeros((BLOCK, LANES), _BF16)
    zk = jnp.zeros((nkeys, LANES), _BF16)
    eye = eye_ref[...]
    ones_bd = ones_ref[...]

    def units(g, pjs):
        q_ref, kp_ref, kc_ref, kn_ref, vp_ref, vc_ref, vn_ref = in_refs[7 * g:7 * g + 7]
        nj = ts // B_GROUPS[g][1] // BLOCK
        work = [(p, j, slice(pr * LANES, (pr + 1) * LANES)) for p, j in pjs for pr in range(2)]

        def scores(p, j, cols):
            bandt = bandt_ref[_edge_variant(t, j, pl.num_programs(1), nj)]
            qp = q_ref[0, p, j * BLOCK:(j + 1) * BLOCK, cols]
            kp = _halo_window(kp_ref, kc_ref, kn_ref, p, j, cols)
            qst = jnp.concatenate([jnp.where(lt64_q, qp, zq), jnp.where(lt64_q, zq, qp)], axis=0)
            lhs = jnp.concatenate([qst, eye], axis=1)
            rhs = jnp.concatenate([kp, bandt], axis=1)
            return lax.dot_general(lhs, rhs, _CONTRACT_LAST, preferred_element_type=_F32)

        def finish(sc, p, j, cols):
            m = jnp.max(sc, axis=1, keepdims=True)
            eb = jnp.exp2((sc - m).astype(_BF16))
            pcat = jnp.concatenate([eb[:BLOCK], eb[BLOCK:]], axis=1)
            vp = _halo_window(vp_ref, vc_ref, vn_ref, p, j, cols)
            vbd = jnp.concatenate([jnp.where(lt64_k, vp, zk), jnp.where(lt64_k, zk, vp)], axis=0)
            ol = jnp.dot(pcat, jnp.concatenate([vbd, ones_bd], axis=1), preferred_element_type=_F32)
            return ol[:, :LANES], jnp.where(lt64_q, m[:BLOCK], m[BLOCK:]), ol[:, LANES:]

        scs, res = [], []
        for i in range(len(work) + LOOKAHEAD_B):
            if i < len(work):
                scs.append(scores(*work[i]))
            if i >= LOOKAHEAD_B:
                res.append(finish(scs[i - LOOKAHEAD_B], *work[i - LOOKAHEAD_B]))
        return [res[2 * i:2 * i + 2] for i in range(len(pjs))]

    for g in range(1, ng):
        dil = B_GROUPS[g][1]
        nj = ts // dil // BLOCK
        for u0 in range(0, dil * nj, UNITS_B):
            pjs = [((u0 + i) // nj, (u0 + i) % nj) for i in range(UNITS_B)]
            for (p, j), res in zip(pjs, units(g, pjs)):
                if dil == PHASE_MAJOR_DIL:
                    rows = pl.ds(p * SLAB_PITCH, BLOCK)
                else:
                    rows = pl.ds(p + j * (BLOCK * dil), BLOCK, stride=dil)
                for pr in range(2):
                    for kind in range(3):
                        slab_refs[g][kind * 2 + pr, rows, :] = res[pr][kind]

    def slab_block(g, idx, j):
        dil = B_GROUPS[g][1]
        if dil != PHASE_MAJOR_DIL:
            return slab_refs[g][idx, j * BLOCK:(j + 1) * BLOCK, :]
        per = BLOCK // dil
        pieces = [slab_refs[g][idx, pl.ds((SUBLANES * (r % 2)) * SLAB_PITCH + j * per + r // 2, SUBLANES,
                                          stride=SLAB_PITCH), :]
                  for r in range(BLOCK // SUBLANES)]
        return jnp.concatenate(pieces, axis=0)

    for j0 in range(0, ts // BLOCK, UNITS_FINAL):
        pjs = [(0, j0 + i) for i in range(UNITS_FINAL)]
        for (_, j), res in zip(pjs, units(0, pjs)):
            for pr in range(2):
                os_, ms_, ls_ = [res[pr][0]], [res[pr][1]], [res[pr][2]]
                for g in range(1, ng):
                    os_.append(slab_block(g, _SLAB_O * 2 + pr, j))
                    ms_.append(slab_block(g, _SLAB_M * 2 + pr, j))
                    ls_.append(slab_block(g, _SLAB_L * 2 + pr, j))
                mx = functools.reduce(jnp.maximum, ms_)
                num = 0.0
                den = 0.0
                for o, m, l in zip(os_, ms_, ls_):
                    a = jnp.exp2(m - mx)
                    num = num + a * o
                    den = den + a * l
                o_ref[0, j * BLOCK:(j + 1) * BLOCK, pr * LANES:(pr + 1) * LANES] = (num / den).astype(o_ref.dtype)


def _attn_b(qkv, bandt, eye, ones_bd):
    b = qkv[0][0].shape[0]
    s = qkv[0][0].shape[2]
    ts = TS_B
    nt_tiles = s // ts
    args, in_specs = [], []
    for (q, k, v), (_, dil) in zip(qkv, B_GROUPS):
        nt = ts // dil
        nh = nt // B_SIDE
        last = s // dil // B_SIDE - 1
        cur = pl.BlockSpec((1, dil, nt, B_GW), lambda i, j: (i, 0, j, 0))
        prev = pl.BlockSpec((1, dil, B_SIDE, B_GW), lambda i, j, nh=nh: (i, 0, jnp.maximum(j * nh - 1, 0), 0))
        nxt = pl.BlockSpec((1, dil, B_SIDE, B_GW),
                           lambda i, j, nh=nh, last=last: (i, 0, jnp.minimum((j + 1) * nh, last), 0))
        args += [q, k, k, k, v, v, v]
        in_specs += [cur, prev, cur, nxt, prev, cur, nxt]
    slab_shapes = []
    for _, dil in B_GROUPS[1:]:
        rows = dil * SLAB_PITCH if dil == PHASE_MAJOR_DIL else ts
        slab_shapes.append(pltpu.VMEM((3 * 2, rows, LANES), _F32))
    return pl.pallas_call(
        _attn_b_kernel,
        out_shape=jax.ShapeDtypeStruct((b, s, B_GW), _BF16),
        grid=(b, nt_tiles),
        in_specs=in_specs + [_const_spec(bandt.shape), _const_spec(eye.shape), _const_spec(ones_bd.shape)],
        out_specs=pl.BlockSpec((1, ts, B_GW), lambda i, j: (i, j, 0)),
        scratch_shapes=slab_shapes,
        compiler_params=pltpu.CompilerParams(
            dimension_semantics=("arbitrary", "arbitrary"), vmem_limit_bytes=VMEM_LIMIT),
        name="attn_b",
    )(*args, bandt, eye, ones_bd)


def _layer_norm(v, g, b):
    mu = jnp.mean(v, axis=-1, keepdims=True)
    d = v - mu
    var = jnp.mean(d * d, axis=-1, keepdims=True)
    return d * lax.rsqrt(var + LN_EPS) * g + b


def _after_stores(val, ref, rows):
    probe = jnp.sum(ref[rows, 0:LANES].reshape(-1, SUBLANES, LANES), axis=0)
    n = SUBLANES * (WORD_BYTES // val.dtype.itemsize)
    probe = jnp.concatenate([probe] * (n // SUBLANES), axis=0).astype(val.dtype)
    head = jnp.where(pl.program_id(0) < 0, probe, val[0:n, 0:LANES])
    top = jnp.concatenate([head, val[0:n, LANES:]], axis=1)
    return jnp.concatenate([top, val[n:]], axis=0)


def _tail_kernel(x_ref, oa_ref, ob_ref, wg_ref, bg_ref, wa_ref, wb_ref, wo_ref, g1_ref, b1_ref,
                 wfg_ref, wfu_ref, wfd_ref, g2_ref, b2_ref, o_ref):
    half = x_ref.shape[0] // 2
    g1, b1, g2, b2 = g1_ref[...], b1_ref[...], g2_ref[...], b2_ref[...]

    def mixer_out(rows):
        xb = x_ref[rows, :].astype(_BF16)
        gates = jax.nn.sigmoid(jnp.dot(xb, wg_ref[...], preferred_element_type=_F32) + bg_ref[...])
        pa = jnp.dot(oa_ref[rows, :], wa_ref[...], preferred_element_type=_F32)
        pb = jnp.dot(ob_ref[rows, :], wb_ref[...], preferred_element_type=_F32)
        mixed = (gates[:, :D_MODEL] * pa + gates[:, D_MODEL:] * pb).astype(_BF16)
        y = jnp.dot(mixed, wo_ref[...], preferred_element_type=_F32)
        return _layer_norm(DN_ALPHA * x_ref[rows, :] + y, g1, b1)

    def activations(h1):
        hb = h1.astype(_BF16)
        gt = jnp.dot(hb, wfg_ref[...], preferred_element_type=_F32)
        up = jnp.dot(hb, wfu_ref[...], preferred_element_type=_F32)
        return (jax.nn.silu(gt) * up).astype(_BF16)

    def finish(act, h1, rows):
        f = jnp.dot(act, wfd_ref[...], preferred_element_type=_F32)
        o_ref[rows, :] = _layer_norm(DN_ALPHA * h1 + f, g2, b2)

    first, second = slice(0, half), slice(half, 2 * half)
    h1_a = mixer_out(first)
    h1_b = mixer_out(second)
    finish(activations(h1_a), h1_a, first)
    finish(_after_stores(activations(h1_b), o_ref, first), h1_b, second)


def _tail(x2, oa2, ob2, consts):
    n = x2.shape[0]
    tm = TM_DENSE
    row = lambda w: pl.BlockSpec((tm, w), lambda i: (i, 0))
    return pl.pallas_call(
        _tail_kernel,
        out_shape=jax.ShapeDtypeStruct((n, D_MODEL), _F32),
        grid=(n // tm,),
        in_specs=[row(D_MODEL), row(A_Q_W), row(B_GW)] + [_const_spec(c.shape) for c in consts],
        out_specs=row(D_MODEL),
        compiler_params=pltpu.CompilerParams(
            dimension_semantics=("arbitrary",), vmem_limit_bytes=VMEM_LIMIT),
        name="tail",
    )(x2, oa2, ob2, *consts)


def _rope_tables(s):
    half = ROT_DIM // 2
    inv = np.float32(ROPE_THETA) ** (-np.arange(half, dtype=np.float32) / np.float32(half))
    ang = np.arange(s).astype(np.float32)[:, None] * inv[None, :]
    cos, sin = np.cos(ang).astype(np.float32), np.sin(ang).astype(np.float32)
    zeros = lambda w: np.zeros((s, w), np.float32)
    cos_h = np.concatenate([cos, cos, np.ones((s, HEAD_DIM - ROT_DIM), np.float32)], axis=1)
    sa_h = np.concatenate([-sin, zeros(HEAD_DIM - half)], axis=1)
    sb_h = np.concatenate([zeros(half), sin, zeros(HEAD_DIM - ROT_DIM)], axis=1)
    two = lambda a: jnp.asarray(np.concatenate([a, a], axis=1))
    return two(cos_h), two(sa_h), two(sb_h)


def _band_t(nk, halo, width):
    key = np.arange(nk)[:, None]
    rel = key - np.arange(BLOCK)[None, :]
    band = (rel >= 0) & (rel <= width)
    lead, trail = key >= halo, key < nk - halo
    variants = [band, band & lead, band & trail, band & lead & trail]
    return jnp.asarray(np.where(np.stack(variants), 0.0, NEG).astype(np.float32)).astype(_BF16)


def _ones_block_diag(nkeys):
    lo = np.broadcast_to(np.arange(LANES) < HEAD_DIM, (nkeys, LANES))
    return jnp.asarray(np.concatenate([lo, ~lo], axis=0).astype(np.float32)).astype(_BF16)


def _stacked_eye(heads):
    return jnp.asarray(np.tile(np.eye(BLOCK, dtype=np.float32), (heads, 1))).astype(_BF16)


def kernel(x, w_in, a_sink, w_gate, b_gate, w_br_a, w_br_b, w_out, ln1_g, ln1_b,
           w_ff_gate, w_ff_up, w_ff_down, ln2_g, ln2_b):
    b, s, d = x.shape
    assert d == D_MODEL and s % TS_B == 0 and s % TQ_A == 0 and s % TM_PROJ == 0
    assert w_in.shape[0] == DEPTH == 1
    cos_t, sa_t, sb_t = _rope_tables(s)
    bf = lambda w: w[0].astype(_BF16)
    row = lambda v: v[0].astype(_F32)[None, :]

    outs = _proj(x, bf(w_in), cos_t, sa_t, sb_t)
    qa, ka2, va2 = outs[:3]
    qkv = [tuple(outs[3 + 3 * g:6 + 3 * g]) for g in range(len(B_GROUPS))]
    o_a = _attn_a(qa, ka2, va2, a_sink[0].astype(_F32),
                  _band_t(3 * BLOCK, BLOCK, 2 * A_WINDOW), _stacked_eye(A_Q_HEADS // A_KV_HEADS),
                  _ones_block_diag(3 * BLOCK))
    o_b = _attn_b(qkv, _band_t(BLOCK + 2 * B_SIDE, B_SIDE, 2 * B_SIDE), _stacked_eye(2),
                  _ones_block_diag(BLOCK + 2 * B_SIDE))

    n = b * s
    out = _tail(x.reshape(n, d), o_a.reshape(n, A_Q_W), o_b.reshape(n, B_GW),
                [bf(w_gate), row(b_gate), bf(w_br_a), bf(w_br_b), bf(w_out), row(ln1_g), row(ln1_b),
                 bf(w_ff_gate), bf(w_ff_up), bf(w_ff_down), row(ln2_g), row(ln2_b)])
    return out.reshape(b, s, d)
```

```python
import functools

import numpy as np
import jax
import jax.numpy as jnp
from jax import lax
from jax.experimental import pallas as pl
from jax.experimental.pallas import tpu as pltpu

D_MODEL = 1024
HEAD_DIM = 64
ROT_DIM = HEAD_DIM // 4
ROPE_THETA = 500000.0
BLOCK = 128
A_Q_HEADS = 16
A_KV_HEADS = 4
A_WINDOW = 128
B_GROUPS = ((128, 1), (512, 4), (2048, 16))
B_HEADS_PER_GROUP = 4
B_SIDE = 64
A_Q_W = A_Q_HEADS * HEAD_DIM
A_KV_W = A_KV_HEADS * HEAD_DIM
B_GW = B_HEADS_PER_GROUP * HEAD_DIM
B_W = len(B_GROUPS) * B_GW
OFF_KA = A_Q_W
OFF_VA = OFF_KA + A_KV_W
OFF_QB = OFF_VA + A_KV_W
OFF_KB = OFF_QB + B_W
OFF_VB = OFF_KB + B_W
DEPTH = 1
DN_ALPHA = (2 * DEPTH) ** 0.25
LN_EPS = 1e-5
NEG = -1e30
LOG2E = 1.4426950408889634
LANES = 128
SUBLANES = 8
WORD_BYTES = 4
VMEM_LIMIT = 56 * 1024 * 1024

TM_PROJ = 1024
TQ_A = 2048
TS_B = 2048
TM_DENSE = 512
LN_ROWS = 32
LOOKAHEAD_A = 2
LOOKAHEAD_B = 4
UNITS_B = 16
UNITS_FINAL = 8
PHASE_MAJOR_DIL = 16
SLAB_PITCH = BLOCK + SUBLANES

_BF16 = jnp.bfloat16
_F32 = jnp.float32
_CONTRACT_LAST = (((1,), (1,)), ((), ()))


def _const_spec(shape):
    nd = len(shape)
    return pl.BlockSpec(shape, lambda *_: (0,) * nd, pipeline_mode=pl.Buffered(1))


def _lane_lt64(rows):
    return lax.broadcasted_iota(jnp.int32, (rows, LANES), 1) < HEAD_DIM


def _edge_variant(tile, blk, n_tiles, n_blks):
    first = jnp.logical_and(tile == 0, blk == 0)
    last = jnp.logical_and(tile == n_tiles - 1, blk == n_blks - 1)
    return first.astype(jnp.int32) + 2 * last.astype(jnp.int32)


def _rope(t, cos, sa, sb):
    return t * cos + pltpu.roll(t, LANES - ROT_DIM // 2, 1) * sa + pltpu.roll(t, ROT_DIM // 2, 1) * sb


def _proj_kernel(x_ref, w_ref, cos_ref, sa_ref, sb_ref,
                 qa_ref, ka_ref, va_ref,
                 q0_ref, k0_ref, v0_ref, q1_ref, k1_ref, v1_ref, q2_ref, k2_ref, v2_ref,
                 slab_ref):
    tm = x_ref.shape[1]
    xb = x_ref[0].astype(_BF16)
    cos = cos_ref[...]
    sa = sa_ref[...]
    sb = sb_ref[...]
    scale = HEAD_DIM ** -0.5 * LOG2E
    lt64 = _lane_lt64(tm)

    def mm(c0, width=2 * LANES):
        return jnp.dot(xb, w_ref[:, c0:c0 + width], preferred_element_type=_F32)

    def mixer_a_q(c):
        r = mm(c * 2 * LANES)
        for h in range(2):
            t = _rope(r[:, h * LANES:(h + 1) * LANES], cos, sa, sb) * scale
            qa_ref[0, :, c * 2 * LANES + h * LANES:c * 2 * LANES + (h + 1) * LANES] = t.astype(_BF16)

    def mixer_a_kv(off, out_ref, rot):
        r = mm(off)
        for h in range(2):
            t = r[:, h * LANES:(h + 1) * LANES]
            if rot:
                t = _rope(t, cos, sa, sb)
            sw = pltpu.roll(t, HEAD_DIM, 1)
            out_ref[0, :, (2 * h) * LANES:(2 * h + 1) * LANES] = jnp.where(lt64, t, sw).astype(_BF16)
            out_ref[0, :, (2 * h + 1) * LANES:(2 * h + 2) * LANES] = jnp.where(lt64, sw, t).astype(_BF16)

    outs = ((q0_ref, k0_ref, v0_ref), (q1_ref, k1_ref, v1_ref), (q2_ref, k2_ref, v2_ref))

    def mixer_b(g, kind):
        dil = B_GROUPS[g][1]
        out_ref = outs[g][kind]
        slab = 2 * (3 * (g - 1) + kind)
        r = mm((OFF_QB, OFF_KB, OFF_VB)[kind] + g * B_GW)
        for h in range(2):
            t = r[:, h * LANES:(h + 1) * LANES]
            if kind < 2:
                t = _rope(t, cos, sa, sb)
            if kind == 0:
                t = t * scale
            if dil == 1:
                out_ref[0, 0, :, h * LANES:(h + 1) * LANES] = t.astype(_BF16)
            elif dil % 16:
                slab_ref[slab + h, 0:tm] = t
            else:
                for k in range(tm // dil):
                    slab_ref[slab + h, k * (dil + SUBLANES):k * (dil + SUBLANES) + dil] = t[k * dil:(k + 1) * dil]
        if dil > 1:
            rows = tm // dil
            pitch = dil if dil % 16 else dil + SUBLANES
            for h in range(2):
                for p in range(dil):
                    v = slab_ref[slab + h, pl.ds(p, rows, stride=pitch), :]
                    out_ref[0, p, :, h * LANES:(h + 1) * LANES] = v.astype(_BF16)

    for c in range(A_Q_W // (2 * LANES)):
        mixer_a_q(c)
    mixer_a_kv(OFF_KA, ka_ref, True)
    mixer_a_kv(OFF_VA, va_ref, False)
    for g in range(len(B_GROUPS)):
        for kind in range(3):
            mixer_b(g, kind)


def _proj(x, w_in_bf, cos_t, sa_t, sb_t):
    b, s, _ = x.shape
    tm = TM_PROJ
    grid = (s // tm, b)
    tok = lambda i, j: (j, i, 0)
    tab = pl.BlockSpec((tm, LANES), lambda i, j: (i, 0))
    out_shape = [jax.ShapeDtypeStruct((b, s, A_Q_W), _BF16),
                 jax.ShapeDtypeStruct((b, s, 2 * A_KV_W), _BF16),
                 jax.ShapeDtypeStruct((b, s, 2 * A_KV_W), _BF16)]
    out_specs = [pl.BlockSpec((1, tm, A_Q_W), tok),
                 pl.BlockSpec((1, tm, 2 * A_KV_W), tok),
                 pl.BlockSpec((1, tm, 2 * A_KV_W), tok)]
    for _, dil in B_GROUPS:
        for _ in range(3):
            out_shape.append(jax.ShapeDtypeStruct((b, dil, s // dil, B_GW), _BF16))
            out_specs.append(pl.BlockSpec((1, dil, tm // dil, B_GW), lambda i, j: (j, 0, i, 0)))
    return pl.pallas_call(
        _proj_kernel,
        out_shape=out_shape,
        grid=grid,
        in_specs=[pl.BlockSpec((1, tm, D_MODEL), tok), _const_spec(w_in_bf.shape), tab, tab, tab],
        out_specs=out_specs,
        scratch_shapes=[pltpu.VMEM((2 * 3 * (len(B_GROUPS) - 1), tm + tm // 2, LANES), _F32)],
        compiler_params=pltpu.CompilerParams(
            dimension_semantics=("arbitrary", "arbitrary"), vmem_limit_bytes=VMEM_LIMIT),
        name="proj",
    )(x, w_in_bf, cos_t, sa_t, sb_t)


def _attn_a_kernel(sink_ref, q_ref, kp_ref, kc_ref, kn_ref, vp_ref, vc_ref, vn_ref, bandt_ref, eye_ref,
                   ones_ref, o_ref):
    tq = q_ref.shape[1]
    t = pl.program_id(1)
    nblk = tq // BLOCK
    nkeys = 3 * BLOCK
    grp = A_Q_HEADS // A_KV_HEADS
    lt64_q = _lane_lt64(BLOCK)
    lt64_k = _lane_lt64(nkeys)
    zq = jnp.zeros((BLOCK, LANES), _BF16)
    zk = jnp.zeros((nkeys, LANES), _BF16)
    eye = eye_ref[...]
    ones_bd = ones_ref[...]

    def window(prev_ref, cur_ref, next_ref, j, k):
        cols = slice(k * LANES, (k + 1) * LANES)
        pieces = []
        if j == 0:
            pieces.append(prev_ref[0, :, cols])
        pieces.append(cur_ref[0, max(j - 1, 0) * BLOCK:min(j + 2, nblk) * BLOCK, cols])
        if j == nblk - 1:
            pieces.append(next_ref[0, :, cols])
        return pieces[0] if len(pieces) == 1 else jnp.concatenate(pieces, axis=0)

    def scores(j, k):
        rows = slice(j * BLOCK, (j + 1) * BLOCK)
        bandt = bandt_ref[_edge_variant(t, j, pl.num_programs(1), nblk)]
        kd = window(kp_ref, kc_ref, kn_ref, j, k)
        qs = []
        for pr in range(2):
            qp = q_ref[0, rows, (2 * k + pr) * LANES:(2 * k + pr + 1) * LANES]
            qs.append(jnp.where(lt64_q, qp, zq))
            qs.append(jnp.where(lt64_q, zq, qp))
        lhs = jnp.concatenate([jnp.concatenate(qs, axis=0), eye], axis=1)
        rhs = jnp.concatenate([kd, bandt], axis=1)
        return lax.dot_general(lhs, rhs, _CONTRACT_LAST, preferred_element_type=_F32)

    def finish(sc, j, k):
        ps, sinks = [], []
        for i in range(grp):
            si = sc[i * BLOCK:(i + 1) * BLOCK]
            sk = sink_ref[grp * k + i] * LOG2E
            m = jnp.maximum(jnp.max(si, axis=1, keepdims=True), sk)
            ps.append(jnp.exp2((si - m).astype(_BF16)))
            sinks.append(jnp.exp2(sk - m))
        pst = jnp.concatenate([jnp.concatenate([ps[0], ps[1]], axis=1),
                               jnp.concatenate([ps[2], ps[3]], axis=1)], axis=0)
        vd = window(vp_ref, vc_ref, vn_ref, j, k)
        vbd = jnp.concatenate([jnp.where(lt64_k, vd, zk), jnp.where(lt64_k, zk, vd)], axis=0)
        ol = jnp.dot(pst, jnp.concatenate([vbd, ones_bd], axis=1), preferred_element_type=_F32)
        for pr in range(2):
            rows = slice(pr * BLOCK, (pr + 1) * BLOCK)
            den = ol[rows, LANES:] + jnp.where(lt64_q, sinks[2 * pr], sinks[2 * pr + 1])
            res = ol[rows, :LANES] / den
            o_ref[0, j * BLOCK:(j + 1) * BLOCK, (2 * k + pr) * LANES:(2 * k + pr + 1) * LANES] = res.astype(o_ref.dtype)

    work = [(j, k) for j in range(nblk) for k in range(A_KV_HEADS)]
    scs = []
    for i in range(len(work) + LOOKAHEAD_A):
        if i < len(work):
            scs.append(scores(*work[i]))
        if i >= LOOKAHEAD_A:
            finish(scs[i - LOOKAHEAD_A], *work[i - LOOKAHEAD_A])


def _attn_a(qa, ka2, va2, sink, bandt, eye, ones_bd):
    b, s, _ = qa.shape
    tq = TQ_A
    nb = tq // BLOCK
    last = s // BLOCK - 1
    cur = lambda i, j: (i, j, 0)
    prev = lambda i, j: (i, jnp.maximum(j * nb - 1, 0), 0)
    nxt = lambda i, j: (i, jnp.minimum((j + 1) * nb, last), 0)
    kvw = ka2.shape[-1]
    kv_specs = [pl.BlockSpec((1, BLOCK, kvw), prev), pl.BlockSpec((1, tq, kvw), cur),
                pl.BlockSpec((1, BLOCK, kvw), nxt)]
    return pl.pallas_call(
        _attn_a_kernel,
        out_shape=jax.ShapeDtypeStruct((b, s, A_Q_W), _BF16),
        grid=(b, s // tq),
        in_specs=[pl.BlockSpec(memory_space=pltpu.SMEM),
                  pl.BlockSpec((1, tq, A_Q_W), cur)] + kv_specs + kv_specs
                 + [_const_spec(bandt.shape), _const_spec(eye.shape), _const_spec(ones_bd.shape)],
        out_specs=pl.BlockSpec((1, tq, A_Q_W), cur),
        compiler_params=pltpu.CompilerParams(
            dimension_semantics=("arbitrary", "arbitrary"), vmem_limit_bytes=VMEM_LIMIT),
        name="attn_a",
    )(sink, qa, ka2, ka2, ka2, va2, va2, va2, bandt, eye, ones_bd)


_SLAB_O, _SLAB_M, _SLAB_L = 0, 1, 2


def _halo_window(prev_ref, cur_ref, next_ref, p, j, cols):
    nt = cur_ref.shape[2]
    lo, hi = j * BLOCK - B_SIDE, (j + 1) * BLOCK + B_SIDE
    pieces = []
    if lo < 0:
        pieces.append(prev_ref[0, p, :, cols])
    pieces.append(cur_ref[0, p, max(lo, 0):min(hi, nt), cols])
    if hi > nt:
        pieces.append(next_ref[0, p, :, cols])
    return pieces[0] if len(pieces) == 1 else jnp.concatenate(pieces, axis=0)


def _attn_b_kernel(*refs):
    ng = len(B_GROUPS)
    in_refs = refs[:7 * ng]
    bandt_ref, eye_ref, ones_ref, o_ref = refs[7 * ng:7 * ng + 4]
    slab_refs = (None,) + tuple(refs[7 * ng + 4:])
    ts = o_ref.shape[1]
    t = pl.program_id(1)
    nkeys = BLOCK + 2 * B_SIDE
    lt64_q = _lane_lt64(BLOCK)
    lt64_k = _lane_lt64(nkeys)
    zq = jnp.zeros((BLOCK, LANES), _BF16)
    zk = jnp.zeros((nkeys, LANES), _BF16)
    eye = eye_ref[...]
    ones_bd = ones_ref[...]

    def units(g, pjs):
        q_ref, kp_ref, kc_ref, kn_ref, vp_ref, vc_ref, vn_ref = in_refs[7 * g:7 * g + 7]
        nj = ts // B_GROUPS[g][1] // BLOCK
        work = [(p, j, slice(pr * LANES, (pr + 1) * LANES)) for p, j in pjs for pr in range(2)]

        def scores(p, j, cols):
            bandt = bandt_ref[_edge_variant(t, j, pl.num_programs(1), nj)]
            qp = q_ref[0, p, j * BLOCK:(j + 1) * BLOCK, cols]
            kp = _halo_window(kp_ref, kc_ref, kn_ref, p, j, cols)
            qst = jnp.concatenate([jnp.where(lt64_q, qp, zq), jnp.where(lt64_q, zq, qp)], axis=0)
            lhs = jnp.concatenate([qst, eye], axis=1)
            rhs = jnp.concatenate([kp, bandt], axis=1)
            return lax.dot_general(lhs, rhs, _CONTRACT_LAST, preferred_element_type=_F32)

        def finish(sc, p, j, cols):
            m = jnp.max(sc, axis=1, keepdims=True)
            eb = jnp.exp2((sc - m).astype(_BF16))
            pcat = jnp.concatenate([eb[:BLOCK], eb[BLOCK:]], axis=1)
            vp = _halo_window(vp_ref, vc_ref, vn_ref, p, j, cols)
            vbd = jnp.concatenate([jnp.where(lt64_k, vp, zk), jnp.where(lt64_k, zk, vp)], axis=0)
            ol = jnp.dot(pcat, jnp.concatenate([vbd, ones_bd], axis=1), preferred_element_type=_F32)
            return ol[:, :LANES], jnp.where(lt64_q, m[:BLOCK], m[BLOCK:]), ol[:, LANES:]

        scs, res = [], []
        for i in range(len(work) + LOOKAHEAD_B):
            if i < len(work):
                scs.append(scores(*work[i]))
            if i >= LOOKAHEAD_B:
                res.append(finish(scs[i - LOOKAHEAD_B], *work[i - LOOKAHEAD_B]))
        return [res[2 * i:2 * i + 2] for i in range(len(pjs))]

    for g in range(1, ng):
        dil = B_GROUPS[g][1]
        nj = ts // dil // BLOCK
        for u0 in range(0, dil * nj, UNITS_B):
            pjs = [((u0 + i) // nj, (u0 + i) % nj) for i in range(UNITS_B)]
            for (p, j), res in zip(pjs, units(g, pjs)):
                if dil == PHASE_MAJOR_DIL:
                    rows = pl.ds(p * SLAB_PITCH, BLOCK)
                else:
                    rows = pl.ds(p + j * (BLOCK * dil), BLOCK, stride=dil)
                for pr in range(2):
                    for kind in range(3):
                        slab_refs[g][kind * 2 + pr, rows, :] = res[pr][kind]

    def slab_block(g, idx, j):
        dil = B_GROUPS[g][1]
        if dil != PHASE_MAJOR_DIL:
            return slab_refs[g][idx, j * BLOCK:(j + 1) * BLOCK, :]
        per = BLOCK // dil
        pieces = [slab_refs[g][idx, pl.ds((SUBLANES * (r % 2)) * SLAB_PITCH + j * per + r // 2, SUBLANES,
                                          stride=SLAB_PITCH), :]
                  for r in range(BLOCK // SUBLANES)]
        return jnp.concatenate(pieces, axis=0)

    for j0 in range(0, ts // BLOCK, UNITS_FINAL):
        pjs = [(0, j0 + i) for i in range(UNITS_FINAL)]
        for (_, j), res in zip(pjs, units(0, pjs)):
            for pr in range(2):
                os_, ms_, ls_ = [res[pr][0]], [res[pr][1]], [res[pr][2]]
                for g in range(1, ng):
                    os_.append(slab_block(g, _SLAB_O * 2 + pr, j))
                    ms_.append(slab_block(g, _SLAB_M * 2 + pr, j))
                    ls_.append(slab_block(g, _SLAB_L * 2 + pr, j))
                mx = functools.reduce(jnp.maximum, ms_)
                num = 0.0
                den = 0.0
                for o, m, l in zip(os_, ms_, ls_):
                    a = jnp.exp2(m - mx)
                    num = num + a * o
                    den = den + a * l
                o_ref[0, j * BLOCK:(j + 1) * BLOCK, pr * LANES:(pr + 1) * LANES] = (num / den).astype(o_ref.dtype)


def _attn_b(qkv, bandt, eye, ones_bd):
    b = qkv[0][0].shape[0]
    s = qkv[0][0].shape[2]
    ts = TS_B
    nt_tiles = s // ts
    args, in_specs = [], []
    for (q, k, v), (_, dil) in zip(qkv, B_GROUPS):
        nt = ts // dil
        nh = nt // B_SIDE
        last = s // dil // B_SIDE - 1
        cur = pl.BlockSpec((1, dil, nt, B_GW), lambda i, j: (i, 0, j, 0))
        prev = pl.BlockSpec((1, dil, B_SIDE, B_GW), lambda i, j, nh=nh: (i, 0, jnp.maximum(j * nh - 1, 0), 0))
        nxt = pl.BlockSpec((1, dil, B_SIDE, B_GW),
                           lambda i, j, nh=nh, last=last: (i, 0, jnp.minimum((j + 1) * nh, last), 0))
        args += [q, k, k, k, v, v, v]
        in_specs += [cur, prev, cur, nxt, prev, cur, nxt]
    slab_shapes = []
    for _, dil in B_GROUPS[1:]:
        rows = dil * SLAB_PITCH if dil == PHASE_MAJOR_DIL else ts
        slab_shapes.append(pltpu.VMEM((3 * 2, rows, LANES), _F32))
    return pl.pallas_call(
        _attn_b_kernel,
        out_shape=jax.ShapeDtypeStruct((b, s, B_GW), _BF16),
        grid=(b, nt_tiles),
        in_specs=in_specs + [_const_spec(bandt.shape), _const_spec(eye.shape), _const_spec(ones_bd.shape)],
        out_specs=pl.BlockSpec((1, ts, B_GW), lambda i, j: (i, j, 0)),
        scratch_shapes=slab_shapes,
        compiler_params=pltpu.CompilerParams(
            dimension_semantics=("arbitrary", "arbitrary"), vmem_limit_bytes=VMEM_LIMIT),
        name="attn_b",
    )(*args, bandt, eye, ones_bd)


def _layer_norm(v, g, b):
    mu = jnp.mean(v, axis=-1, keepdims=True)
    d = v - mu
    var = jnp.mean(d * d, axis=-1, keepdims=True)
    return d * lax.rsqrt(var + LN_EPS) * g + b


def _after_stores(val, ref, rows):
    probe = jnp.sum(ref[rows, 0:LANES].reshape(-1, SUBLANES, LANES), axis=0)
    n = SUBLANES * (WORD_BYTES // val.dtype.itemsize)
    probe = jnp.concatenate([probe] * (n // SUBLANES), axis=0).astype(val.dtype)
    head = jnp.where(pl.program_id(0) < 0, probe, val[0:n, 0:LANES])
    top = jnp.concatenate([head, val[0:n, LANES:]], axis=1)
    return jnp.concatenate([top, val[n:]], axis=0)


def _tail_kernel(x_ref, oa_ref, ob_ref, wg_ref, bg_ref, wa_ref, wb_ref, wo_ref, g1_ref, b1_ref,
                 wfg_ref, wfu_ref, wfd_ref, g2_ref, b2_ref, o_ref):
    half = x_ref.shape[0] // 2
    g1, b1, g2, b2 = g1_ref[...], b1_ref[...], g2_ref[...], b2_ref[...]

    def mixer_out(rows):
        xb = x_ref[rows, :].astype(_BF16)
        gates = jax.nn.sigmoid(jnp.dot(xb, wg_ref[...], preferred_element_type=_F32) + bg_ref[...])
        pa = jnp.dot(oa_ref[rows, :], wa_ref[...], preferred_element_type=_F32)
        pb = jnp.dot(ob_ref[rows, :], wb_ref[...], preferred_element_type=_F32)
        mixed = (gates[:, :D_MODEL] * pa + gates[:, D_MODEL:] * pb).astype(_BF16)
        y = jnp.dot(mixed, wo_ref[...], preferred_element_type=_F32)
        return _layer_norm(DN_ALPHA * x_ref[rows, :] + y, g1, b1)

    def activations(h1):
        hb = h1.astype(_BF16)
        gt = jnp.dot(hb, wfg_ref[...], preferred_element_type=_F32)
        up = jnp.dot(hb, wfu_ref[...], preferred_element_type=_F32)
        return (jax.nn.silu(gt) * up).astype(_BF16)

    def finish(act, h1, rows):
        f = jnp.dot(act, wfd_ref[...], preferred_element_type=_F32)
        o_ref[rows, :] = DN_ALPHA * h1 + f
        for r in range(rows.start, rows.stop, LN_ROWS):
            o_ref[r:r + LN_ROWS, :] = _layer_norm(o_ref[r:r + LN_ROWS, :], g2, b2)

    first, second = slice(0, half), slice(half, 2 * half)
    h1_a = mixer_out(first)
    h1_b = mixer_out(second)
    finish(activations(h1_a), h1_a, first)
    finish(_after_stores(activations(h1_b), o_ref, first), h1_b, second)


def _tail(x2, oa2, ob2, consts):
    n = x2.shape[0]
    tm = TM_DENSE
    row = lambda w: pl.BlockSpec((tm, w), lambda i: (i, 0))
    return pl.pallas_call(
        _tail_kernel,
        out_shape=jax.ShapeDtypeStruct((n, D_MODEL), _F32),
        grid=(n // tm,),
        in_specs=[row(D_MODEL), row(A_Q_W), row(B_GW)] + [_const_spec(c.shape) for c in consts],
        out_specs=row(D_MODEL),
        compiler_params=pltpu.CompilerParams(
            dimension_semantics=("arbitrary",), vmem_limit_bytes=VMEM_LIMIT),
        name="tail",
    )(x2, oa2, ob2, *consts)


def _rope_tables(s):
    half = ROT_DIM // 2
    inv = np.float32(ROPE_THETA) ** (-np.arange(half, dtype=np.float32) / np.float32(half))
    ang = np.arange(s).astype(np.float32)[:, None] * inv[None, :]
    cos, sin = np.cos(ang).astype(np.float32), np.sin(ang).astype(np.float32)
    zeros = lambda w: np.zeros((s, w), np.float32)
    cos_h = np.concatenate([cos, cos, np.ones((s, HEAD_DIM - ROT_DIM), np.float32)], axis=1)
    sa_h = np.concatenate([-sin, zeros(HEAD_DIM - half)], axis=1)
    sb_h = np.concatenate([zeros(half), sin, zeros(HEAD_DIM - ROT_DIM)], axis=1)
    two = lambda a: jnp.asarray(np.concatenate([a, a], axis=1))
    return two(cos_h), two(sa_h), two(sb_h)


def _band_t(nk, halo, width):
    key = np.arange(nk)[:, None]
    rel = key - np.arange(BLOCK)[None, :]
    band = (rel >= 0) & (rel <= width)
    lead, trail = key >= halo, key < nk - halo
    variants = [band, band & lead, band & trail, band & lead & trail]
    return jnp.asarray(np.where(np.stack(variants), 0.0, NEG).astype(np.float32)).astype(_BF16)


def _ones_block_diag(nkeys):
    lo = np.broadcast_to(np.arange(LANES) < HEAD_DIM, (nkeys, LANES))
    return jnp.asarray(np.concatenate([lo, ~lo], axis=0).astype(np.float32)).astype(_BF16)


def _stacked_eye(heads):
    return jnp.asarray(np.tile(np.eye(BLOCK, dtype=np.float32), (heads, 1))).astype(_BF16)


def kernel(x, w_in, a_sink, w_gate, b_gate, w_br_a, w_br_b, w_out, ln1_g, ln1_b,
           w_ff_gate, w_ff_up, w_ff_down, ln2_g, ln2_b):
    b, s, d = x.shape
    assert d == D_MODEL and s % TS_B == 0 and s % TQ_A == 0 and s % TM_PROJ == 0
    assert w_in.shape[0] == DEPTH == 1
    cos_t, sa_t, sb_t = _rope_tables(s)
    bf = lambda w: w[0].astype(_BF16)
    row = lambda v: v[0].astype(_F32)[None, :]

    outs = _proj(x, bf(w_in), cos_t, sa_t, sb_t)
    qa, ka2, va2 = outs[:3]
    qkv = [tuple(outs[3 + 3 * g:6 + 3 * g]) for g in range(len(B_GROUPS))]
    o_a = _attn_a(qa, ka2, va2, a_sink[0].astype(_F32),
                  _band_t(3 * BLOCK, BLOCK, 2 * A_WINDOW), _stacked_eye(A_Q_HEADS // A_KV_HEADS),
                  _ones_block_diag(3 * BLOCK))
    o_b = _attn_b(qkv, _band_t(BLOCK + 2 * B_SIDE, B_SIDE, 2 * B_SIDE), _stacked_eye(2),
                  _ones_block_diag(BLOCK + 2 * B_SIDE))

    n = b * s
    out = _tail(x.reshape(n, d), o_a.reshape(n, A_Q_W), o_b.reshape(n, B_GW),
                [bf(w_gate), row(b_gate), bf(w_br_a), bf(w_br_b), bf(w_out), row(ln1_g), row(ln1_b),
                 bf(w_ff_gate), bf(w_ff_up), bf(w_ff_down), row(ln2_g), row(ln2_b)])
    return out.reshape(b, s, d)
```
